```python
import jax, jax.numpy as jnp
from jax import lax
import numpy as np

D_MODEL = 1024
BATCH = 4
SEQ = 4096
DEPTH = 1
DEC_BATCH = 1
DEC_SEQ = 16384
PAST_LEN = 128

GRID_W = 64
N_HEADS = 8
N_KV_HEADS = 2
HEAD_DIM = 128
GROUP = N_HEADS // N_KV_HEADS
ATTN_WIDTH = N_HEADS * HEAD_DIM
KV_WIDTH = N_KV_HEADS * HEAD_DIM
ROPE_HALF = HEAD_DIM // 2
ROPE_THETA = 10000.0
Q_BLOCK = 128
CONV_WIDTH = D_MODEL
CONV_GROUPS = 16
CONV_K = 3
N_BRANCH = 2
BRANCH_WIDTH = 1024
FF_HIDDEN = -(-8 * D_MODEL // (3 * 256)) * 256
EPS = 1e-6

IN_SPLITS = (CONV_WIDTH, CONV_WIDTH, CONV_WIDTH, ATTN_WIDTH, KV_WIDTH, KV_WIDTH)
IN_WIDTH = sum(IN_SPLITS) + N_BRANCH * D_MODEL
IN_OFFSETS = tuple(int(v) for v in np.cumsum(IN_SPLITS))

kernel_name = "hybrid_gated_conv_axial_gqa_encoder"


def rmsnorm(x, g):
    xf = x.astype(jnp.float32)
    y = xf * lax.rsqrt(jnp.mean(xf * xf, axis=-1, keepdims=True) + EPS)
    return (y * g.astype(jnp.float32)).astype(x.dtype)


def axial_rope_angles(seq_len):
    n_rows = seq_len // GRID_W
    rows = jnp.repeat(jnp.arange(n_rows, dtype=jnp.float32), GRID_W)
    cols = jnp.tile(jnp.arange(GRID_W, dtype=jnp.float32), n_rows)
    inv_freq = ROPE_THETA ** (-jnp.arange(0, ROPE_HALF, 2, dtype=jnp.float32) / ROPE_HALF)
    ang = jnp.concatenate([rows[:, None] * inv_freq[None, :], cols[:, None] * inv_freq[None, :]], axis=-1)
    return jnp.cos(ang), jnp.sin(ang)


def apply_rope(x, cos, sin):
    b, s, h, d = x.shape
    xp = x.astype(jnp.float32).reshape(b, s, h, d // 2, 2)
    x0, x1 = xp[..., 0], xp[..., 1]
    c = cos[None, :, None, :]
    sn = sin[None, :, None, :]
    out = jnp.stack([x0 * c - x1 * sn, x0 * sn + x1 * c], axis=-1)
    return out.reshape(b, s, h, d).astype(x.dtype)


def centred_depthwise_conv(u, w):
    return lax.conv_general_dilated(
        u, w[:, None, :].astype(u.dtype), window_strides=(1,), padding=[(1, 1)],
        dimension_numbers=("NWC", "WIO", "NWC"), feature_group_count=u.shape[-1])


def gqa_attention(q, k, v):
    b, s = q.shape[0], q.shape[1]
    nb = s // Q_BLOCK
    scale = HEAD_DIM ** -0.5
    qb = q.reshape(b, nb, Q_BLOCK, N_KV_HEADS, GROUP, HEAD_DIM).transpose(1, 0, 2, 3, 4, 5)

    def block(q_blk):
        sc = jnp.einsum("bqkgd,bskd->bkgqs", q_blk, k).astype(jnp.float32) * scale
        p = jax.nn.softmax(sc, axis=-1).astype(v.dtype)
        return jnp.einsum("bkgqs,bskd->bqkgd", p, v)

    out = lax.map(block, qb)
    return out.transpose(1, 0, 2, 3, 4, 5).reshape(b, s, ATTN_WIDTH)


def mixer(xn, w_in, conv_w, q_norm_g, k_norm_g, w_branch, w_out, cos, sin):
    b, s, _ = xn.shape
    proj = xn @ w_in
    c_b, c_c, c_x, q, k, v, gates = jnp.split(proj, IN_OFFSETS, axis=-1)
    y_conv = c_b * centred_depthwise_conv(c_c * c_x, conv_w)
    q = apply_rope(rmsnorm(q.reshape(b, s, N_HEADS, HEAD_DIM), q_norm_g), cos, sin)
    k = apply_rope(rmsnorm(k.reshape(b, s, N_KV_HEADS, HEAD_DIM), k_norm_g), cos, sin)
    v = v.reshape(b, s, N_KV_HEADS, HEAD_DIM)
    y_attn = gqa_attention(q, k, v)
    branches = jnp.stack([y_conv, y_attn], axis=2)
    br = jnp.einsum("bsnc,ncd->bsnd", branches, w_branch)
    g = jax.nn.sigmoid(gates.reshape(b, s, N_BRANCH, D_MODEL))
    merged = jnp.sum(g * br, axis=2)
    return merged @ w_out


def swiglu(xn, w_ffn_in, w_ffn_out):
    gate, up = jnp.split(xn @ w_ffn_in, 2, axis=-1)
    return (jax.nn.silu(gate) * up) @ w_ffn_out


def trunk(x, norm_mix_g, w_in, conv_w, q_norm_g, k_norm_g, w_branch, w_out,
          norm_ffn_g, w_ffn_in, w_ffn_out, norm_final_g):
    cos, sin = axial_rope_angles(x.shape[1])
    for l in range(DEPTH):
        x = x + mixer(rmsnorm(x, norm_mix_g[l]), w_in[l], conv_w[l], q_norm_g[l], k_norm_g[l],
                      w_branch[l], w_out[l], cos, sin)
        x = x + swiglu(rmsnorm(x, norm_ffn_g[l]), w_ffn_in[l], w_ffn_out[l])
    return rmsnorm(x, norm_final_g)


def setup_inputs(seed: int = 0) -> dict:
    key = jax.random.key(seed)
    ks = jax.random.split(key, 13)
    f32 = jnp.float32

    def nrm(k, shape, scale):
        return jax.random.normal(k, shape, f32) * scale

    def gain(k, shape):
        return 1.0 + 0.02 * jax.random.normal(k, shape, f32)

    return {
        "x_prompt": jax.random.normal(ks[0], (BATCH, SEQ, D_MODEL), f32),
        "x_sample": jax.random.normal(ks[1], (DEC_BATCH, DEC_SEQ, D_MODEL), f32),
        "norm_mix_g": gain(ks[2], (DEPTH, D_MODEL)),
        "w_in": nrm(ks[3], (DEPTH, D_MODEL, IN_WIDTH), D_MODEL ** -0.5),
        "conv_w": nrm(ks[4], (DEPTH, CONV_K, CONV_WIDTH), CONV_K ** -0.5),
        "q_norm_g": gain(ks[5], (DEPTH, HEAD_DIM)),
        "k_norm_g": gain(ks[6], (DEPTH, HEAD_DIM)),
        "w_branch": nrm(ks[7], (DEPTH, N_BRANCH, BRANCH_WIDTH, D_MODEL), BRANCH_WIDTH ** -0.5),
        "w_out": nrm(ks[8], (DEPTH, D_MODEL, D_MODEL), D_MODEL ** -0.5),
        "norm_ffn_g": gain(ks[9], (DEPTH, D_MODEL)),
        "w_ffn_in": nrm(ks[10], (DEPTH, D_MODEL, 2 * FF_HIDDEN), D_MODEL ** -0.5),
        "w_ffn_out": nrm(ks[11], (DEPTH, FF_HIDDEN, D_MODEL), FF_HIDDEN ** -0.5),
        "norm_final_g": gain(ks[12], (D_MODEL,)),
    }


def reference(x_prompt, x_sample, norm_mix_g, w_in, conv_w, q_norm_g, k_norm_g, w_branch, w_out,
              norm_ffn_g, w_ffn_in, w_ffn_out, norm_final_g):
    y_prompt = trunk(x_prompt, norm_mix_g, w_in, conv_w, q_norm_g, k_norm_g, w_branch, w_out,
                     norm_ffn_g, w_ffn_in, w_ffn_out, norm_final_g)
    y_sample = trunk(x_sample, norm_mix_g, w_in, conv_w, q_norm_g, k_norm_g, w_branch, w_out,
                     norm_ffn_g, w_ffn_in, w_ffn_out, norm_final_g)
    return (y_prompt, y_sample)
```

```python
import functools

import jax
import jax.numpy as jnp
import numpy as np
from jax import lax
from jax.experimental import pallas as pl
from jax.experimental.pallas import tpu as pltpu

D_MODEL = 1024
GRID_W = 64
N_HEADS = 8
N_KV_HEADS = 2
HEAD_DIM = 128
GROUP = N_HEADS // N_KV_HEADS
ATTN_WIDTH = N_HEADS * HEAD_DIM
KV_WIDTH = N_KV_HEADS * HEAD_DIM
ROPE_HALF = HEAD_DIM // 2
ROPE_THETA = 10000.0
CONV_WIDTH = D_MODEL
N_BRANCH = 2
FF_HIDDEN = 2816
EPS = 1e-6

_V7X_VMEM_BYTES = 64 * 1024 * 1024
_VMEM_LIMIT_BYTES = _V7X_VMEM_BYTES - 8 * 1024 * 1024
_HALO = 16

_PROJ_TM = 512
_PROJ_CK = 512
_ATTN_TQ = 256
_ATTN_TKV = 512
_POST_TM = 256
_FFN_CK = 256

_BF16 = jnp.bfloat16
_F32 = jnp.float32
_NT_DIMS = (((1,), (1,)), ((), ()))


def _dot(a, b):
    return jnp.dot(a, b, preferred_element_type=_F32)


def _dot_nt(a, b):
    return lax.dot_general(a, b, _NT_DIMS, preferred_element_type=_F32)


def _rms_rows(x, g):
    ms = jnp.mean(x * x, axis=-1, keepdims=True)
    return x * lax.rsqrt(ms + EPS) * g


def _resident(shape):
    nd = len(shape)
    return pl.BlockSpec(shape, lambda *_: (0,) * nd, pipeline_mode=pl.Buffered(1))


def _proj_kernel(xm_ref, xp_ref, xn_ref, gmix_ref, wcb_ref, wcc_ref, wcx_ref, wqt_ref, wk_ref,
                 wvt_ref, wg_ref, convw_ref, gq_ref, gk_ref, cost_ref, sint_ref, ck_ref, sk_ref,
                 yconv_ref, qt_ref, k_ref, vt_ref, g_ref, xn_scr, *, tiles_per_seq):
    tm = xm_ref.shape[0]
    i = pl.program_id(0)
    seq_first = (i % tiles_per_seq) == 0
    seq_last = (i % tiles_per_seq) == tiles_per_seq - 1
    gmix = gmix_ref[...]

    prev = _rms_rows(xp_ref[...], gmix)
    nxt = _rms_rows(xn_ref[...], gmix)
    xn_scr[0:_HALO, :] = jnp.where(seq_first, 0.0, prev).astype(_BF16)
    xn_scr[_HALO:_HALO + tm, :] = _rms_rows(xm_ref[...], gmix).astype(_BF16)
    xn_scr[_HALO + tm:, :] = jnp.where(seq_last, 0.0, nxt).astype(_BF16)

    xe = xn_scr[...]
    xm = xn_scr[_HALO:_HALO + tm, :]
    rows_ext = tm + 2 * _HALO

    for c0 in range(0, CONV_WIDTH, _PROJ_CK):
        cs = slice(c0, c0 + _PROJ_CK)
        u = _dot(xe, wcc_ref[:, cs]) * _dot(xe, wcx_ref[:, cs])
        u_prev = pltpu.roll(u, 1, axis=0)
        u_next = pltpu.roll(u, rows_ext - 1, axis=0)
        w = convw_ref[:, cs]
        conv = w[0:1, :] * u_prev + w[1:2, :] * u + w[2:3, :] * u_next
        cb = _dot(xm, wcb_ref[:, cs])
        yconv_ref[:, cs] = (cb * conv[_HALO:_HALO + tm, :]).astype(_BF16)

    cos_t = cost_ref[...]
    sin_t = sint_ref[...]
    gq = gq_ref[...]
    q_scale = HEAD_DIM ** -0.5
    qt = _dot_nt(wqt_ref[...], xm)
    for h in range(N_HEADS):
        r0 = h * HEAD_DIM
        qh = qt[r0:r0 + HEAD_DIM, :]
        ms = jnp.mean(qh * qh, axis=0, keepdims=True)
        qh = qh * (lax.rsqrt(ms + EPS) * q_scale) * gq
        x0 = qh[:ROPE_HALF, :]
        x1 = qh[ROPE_HALF:, :]
        qt_ref[r0:r0 + ROPE_HALF, :] = (x0 * cos_t - x1 * sin_t).astype(_BF16)
        qt_ref[r0 + ROPE_HALF:r0 + HEAD_DIM, :] = (x0 * sin_t + x1 * cos_t).astype(_BF16)

    kf = _dot(xm, wk_ref[...])
    gk = gk_ref[...]
    ck = ck_ref[...]
    sk = sk_ref[...]
    for h in range(N_KV_HEADS):
        cs = slice(h * HEAD_DIM, (h + 1) * HEAD_DIM)
        kh = _rms_rows(kf[:, cs], gk)
        k_ref[:, cs] = (kh * ck + pltpu.roll(kh, ROPE_HALF, axis=1) * sk).astype(_BF16)

    vt_ref[...] = _dot_nt(wvt_ref[...], xm).astype(_BF16)

    for c0 in range(0, N_BRANCH * D_MODEL, _PROJ_CK):
        cs = slice(c0, c0 + _PROJ_CK)
        g_ref[:, cs] = jax.nn.sigmoid(_dot(xm, wg_ref[:, cs])).astype(_BF16)


def _proj_call(x2d, seq_len, gmix, wcb, wcc, wcx, wqt, wk, wvt, wg, convw, gq, gk,
               cos_t, sin_t, ck, sk):
    n_tok = x2d.shape[0]
    tm = _PROJ_TM
    assert seq_len % tm == 0 and tm % _HALO == 0
    tiles_per_seq = seq_len // tm
    halo_per_tile = tm // _HALO
    n_halo_blocks = n_tok // _HALO

    def prev_map(i):
        return (jnp.maximum(i * halo_per_tile - 1, 0), 0)

    def next_map(i):
        return (jnp.minimum((i + 1) * halo_per_tile, n_halo_blocks - 1), 0)

    in_specs = [
        pl.BlockSpec((tm, D_MODEL), lambda i: (i, 0)),
        pl.BlockSpec((_HALO, D_MODEL), prev_map),
        pl.BlockSpec((_HALO, D_MODEL), next_map),
        _resident(gmix.shape), _resident(wcb.shape), _resident(wcc.shape), _resident(wcx.shape),
        _resident(wqt.shape), _resident(wk.shape), _resident(wvt.shape), _resident(wg.shape),
        _resident(convw.shape), _resident(gq.shape), _resident(gk.shape),
        pl.BlockSpec((ROPE_HALF, tm), lambda i: (0, i % tiles_per_seq)),
        pl.BlockSpec((ROPE_HALF, tm), lambda i: (0, i % tiles_per_seq)),
        pl.BlockSpec((tm, HEAD_DIM), lambda i: (i % tiles_per_seq, 0)),
        pl.BlockSpec((tm, HEAD_DIM), lambda i: (i % tiles_per_seq, 0)),
    ]
    out_shape = (
        jax.ShapeDtypeStruct((n_tok, CONV_WIDTH), _BF16),
        jax.ShapeDtypeStruct((ATTN_WIDTH, n_tok), _BF16),
        jax.ShapeDtypeStruct((n_tok, KV_WIDTH), _BF16),
        jax.ShapeDtypeStruct((KV_WIDTH, n_tok), _BF16),
        jax.ShapeDtypeStruct((n_tok, N_BRANCH * D_MODEL), _BF16),
    )
    out_specs = (
        pl.BlockSpec((tm, CONV_WIDTH), lambda i: (i, 0)),
        pl.BlockSpec((ATTN_WIDTH, tm), lambda i: (0, i)),
        pl.BlockSpec((tm, KV_WIDTH), lambda i: (i, 0)),
        pl.BlockSpec((KV_WIDTH, tm), lambda i: (0, i)),
        pl.BlockSpec((tm, N_BRANCH * D_MODEL), lambda i: (i, 0)),
    )
    return pl.pallas_call(
        functools.partial(_proj_kernel, tiles_per_seq=tiles_per_seq),
        grid=(n_tok // tm,),
        in_specs=in_specs,
        out_specs=out_specs,
        out_shape=out_shape,
        scratch_shapes=[pltpu.VMEM((tm + 2 * _HALO, D_MODEL), _BF16)],
        compiler_params=pltpu.CompilerParams(
            dimension_semantics=("arbitrary",), vmem_limit_bytes=_VMEM_LIMIT_BYTES),
        name="proj",
    )(x2d, x2d, x2d, gmix, wcb, wcc, wcx, wqt, wk, wvt, wg, convw, gq, gk, cos_t, sin_t, ck, sk)


def _attn_kernel(qt_ref, k_ref, vt_ref, o_ref):
    tq = qt_ref.shape[1]
    seq_len = k_ref.shape[0]
    tkv = _ATTN_TKV
    n_chunks = seq_len // tkv

    for g in range(GROUP):
        qg = qt_ref[g * HEAD_DIM:(g + 1) * HEAD_DIM, :]

        def body(c, carry, qg=qg):
            m, l, acc = carry
            start = pl.multiple_of(c * tkv, tkv)
            s = _dot(k_ref[pl.ds(start, tkv), :], qg)
            m_new = jnp.maximum(m, jnp.max(s, axis=0, keepdims=True))
            p = jnp.exp(s - m_new)
            alpha = jnp.exp(m - m_new)
            l = alpha * l + jnp.sum(p, axis=0, keepdims=True)
            pv = _dot(vt_ref[:, pl.ds(start, tkv)], p.astype(_BF16))
            return m_new, l, alpha * acc + pv

        init = (jnp.full((1, tq), -1e30, _F32), jnp.zeros((1, tq), _F32),
                jnp.zeros((HEAD_DIM, tq), _F32))
        _, l, acc = lax.fori_loop(0, n_chunks, body, init)
        o_ref[:, g * HEAD_DIM:(g + 1) * HEAD_DIM] = (acc / l).T.astype(_BF16)


def _attn_call(qt, k, vt, batch, seq_len):
    n_tok = batch * seq_len
    tq = _ATTN_TQ
    assert seq_len % tq == 0 and seq_len % _ATTN_TKV == 0
    q_tiles = seq_len // tq
    group_w = GROUP * HEAD_DIM
    return pl.pallas_call(
        _attn_kernel,
        grid=(batch, N_KV_HEADS, q_tiles),
        in_specs=[
            pl.BlockSpec((group_w, tq), lambda b, h, i: (h, b * q_tiles + i)),
            pl.BlockSpec((seq_len, HEAD_DIM), lambda b, h, i: (b, h)),
            pl.BlockSpec((HEAD_DIM, seq_len), lambda b, h, i: (h, b)),
        ],
        out_specs=pl.BlockSpec((tq, group_w), lambda b, h, i: (b * q_tiles + i, h)),
        out_shape=jax.ShapeDtypeStruct((n_tok, ATTN_WIDTH), _BF16),
        compiler_params=pltpu.CompilerParams(
            dimension_semantics=("arbitrary", "arbitrary", "arbitrary"),
            vmem_limit_bytes=_VMEM_LIMIT_BYTES),
        name="attn",
    )(qt, k, vt)


def _post_kernel(x_ref, yc_ref, ya_ref, g_ref, wb0_ref, wb1_ref, wout_ref, gffn_ref, wfg_ref,
                 wfu_ref, wfo_ref, gfin_ref, o_ref, act_scr, *, final_norm):
    g = g_ref[...].astype(_F32)
    merged = g[:, :D_MODEL] * _dot(yc_ref[...], wb0_ref[...])
    merged += g[:, D_MODEL:] * _dot(ya_ref[...], wb1_ref[...])
    x1 = x_ref[...] + _dot(merged.astype(_BF16), wout_ref[...])

    xn = _rms_rows(x1, gffn_ref[...]).astype(_BF16)
    for c0 in range(0, FF_HIDDEN, _FFN_CK):
        cs = slice(c0, c0 + _FFN_CK)
        gate = _dot(xn, wfg_ref[:, cs])
        up = _dot(xn, wfu_ref[:, cs])
        act_scr[:, cs] = (gate * jax.nn.sigmoid(gate) * up).astype(_BF16)
    x2 = x1 + _dot(act_scr[...], wfo_ref[...])
    if final_norm:
        x2 = _rms_rows(x2, gfin_ref[...])
    o_ref[...] = x2


def _post_call(x2d, yconv, yattn, gates, wb0, wb1, wout, gffn, wfg, wfu, wfo, gfin, final_norm):
    n_tok = x2d.shape[0]
    tm = _POST_TM
    assert n_tok % tm == 0 and FF_HIDDEN % _FFN_CK == 0

    def row_spec(width):
        return pl.BlockSpec((tm, width), lambda i: (i, 0))

    return pl.pallas_call(
        functools.partial(_post_kernel, final_norm=final_norm),
        grid=(n_tok // tm,),
        in_specs=[
            row_spec(D_MODEL), row_spec(CONV_WIDTH), row_spec(ATTN_WIDTH),
            row_spec(N_BRANCH * D_MODEL),
            _resident(wb0.shape), _resident(wb1.shape), _resident(wout.shape),
            _resident(gffn.shape), _resident(wfg.shape), _resident(wfu.shape),
            _resident(wfo.shape), _resident(gfin.shape),
        ],
        out_specs=row_spec(D_MODEL),
        out_shape=jax.ShapeDtypeStruct((n_tok, D_MODEL), _F32),
        scratch_shapes=[pltpu.VMEM((tm, FF_HIDDEN), _BF16)],
        compiler_params=pltpu.CompilerParams(
            dimension_semantics=("arbitrary",), vmem_limit_bytes=_VMEM_LIMIT_BYTES),
        name="post",
    )(x2d, yconv, yattn, gates, wb0, wb1, wout, gffn, wfg, wfu, wfo, gfin)


_HEAD_PERM = np.concatenate([np.arange(0, HEAD_DIM, 2), np.arange(1, HEAD_DIM, 2)])


def _permute_heads(w, n_heads):
    return w.reshape(w.shape[0], n_heads, HEAD_DIM)[:, :, _HEAD_PERM].reshape(w.shape[0], -1)


def _layer_weights(w_in, conv_w, q_norm_g, k_norm_g, w_branch, w_out, w_ffn_in, w_ffn_out):
    o = np.cumsum([0, CONV_WIDTH, CONV_WIDTH, CONV_WIDTH, ATTN_WIDTH, KV_WIDTH, KV_WIDTH])
    wb = w_in.astype(_BF16)
    return dict(
        wcb=wb[:, o[0]:o[1]], wcc=wb[:, o[1]:o[2]], wcx=wb[:, o[2]:o[3]],
        wqt=_permute_heads(wb[:, o[3]:o[4]], N_HEADS).T,
        wk=_permute_heads(wb[:, o[4]:o[5]], N_KV_HEADS),
        wvt=wb[:, o[5]:o[6]].T,
        wg=wb[:, o[6]:],
        convw=conv_w,
        gq=q_norm_g[_HEAD_PERM].reshape(HEAD_DIM, 1),
        gk=k_norm_g[_HEAD_PERM].reshape(1, HEAD_DIM),
        wb0=w_branch[0].astype(_BF16), wb1=w_branch[1].astype(_BF16),
        wout=w_out.astype(_BF16),
        wfg=w_ffn_in[:, :FF_HIDDEN].astype(_BF16), wfu=w_ffn_in[:, FF_HIDDEN:].astype(_BF16),
        wfo=w_ffn_out.astype(_BF16),
    )


def _rope_tables(seq_len):
    n_rows = seq_len // GRID_W
    rows = jnp.repeat(jnp.arange(n_rows, dtype=_F32), GRID_W)
    cols = jnp.tile(jnp.arange(GRID_W, dtype=_F32), n_rows)
    inv_freq = ROPE_THETA ** (-jnp.arange(0, ROPE_HALF, 2, dtype=_F32) / ROPE_HALF)
    ang = jnp.concatenate([rows[:, None] * inv_freq[None, :], cols[:, None] * inv_freq[None, :]],
                          axis=-1)
    c, s = jnp.cos(ang), jnp.sin(ang)
    return dict(cos_t=c.T, sin_t=s.T,
                ck=jnp.concatenate([c, c], axis=-1), sk=jnp.concatenate([-s, s], axis=-1))


def _trunk(x, layers, norm_mix_g, norm_ffn_g, norm_final_g):
    batch, seq_len, _ = x.shape
    x2d = x.reshape(batch * seq_len, D_MODEL)
    rope = _rope_tables(seq_len)
    gfin = norm_final_g.reshape(1, D_MODEL)
    for l, w in enumerate(layers):
        yconv, qt, k, vt, gates = _proj_call(
            x2d, seq_len, norm_mix_g[l].reshape(1, D_MODEL), w["wcb"], w["wcc"], w["wcx"],
            w["wqt"], w["wk"], w["wvt"], w["wg"], w["convw"], w["gq"], w["gk"],
            rope["cos_t"], rope["sin_t"], rope["ck"], rope["sk"])
        yattn = _attn_call(qt, k, vt, batch, seq_len)
        x2d = _post_call(x2d, yconv, yattn, gates, w["wb0"], w["wb1"], w["wout"],
                         norm_ffn_g[l].reshape(1, D_MODEL), w["wfg"], w["wfu"], w["wfo"], gfin,
                         final_norm=(l == len(layers) - 1))
    return x2d.reshape(batch, seq_len, D_MODEL)


def kernel(x_prompt, x_sample, norm_mix_g, w_in, conv_w, q_norm_g, k_norm_g, w_branch, w_out,
           norm_ffn_g, w_ffn_in, w_ffn_out, norm_final_g):
    depth = w_in.shape[0]
    layers = [_layer_weights(w_in[l], conv_w[l], q_norm_g[l], k_norm_g[l], w_branch[l], w_out[l],
                             w_ffn_in[l], w_ffn_out[l]) for l in range(depth)]
    y_prompt = _trunk(x_prompt, layers, norm_mix_g, norm_ffn_g, norm_final_g)
    y_sample = _trunk(x_sample, layers, norm_mix_g, norm_ffn_g, norm_final_g)
    return (y_prompt, y_sample)
```

```python
import functools

import jax
import jax.numpy as jnp
import numpy as np
from jax import lax
from jax.experimental import pallas as pl
from jax.experimental.pallas import tpu as pltpu

D_MODEL = 1024
GRID_W = 64
N_HEADS = 8
N_KV_HEADS = 2
HEAD_DIM = 128
GROUP = N_HEADS // N_KV_HEADS
ATTN_WIDTH = N_HEADS * HEAD_DIM
KV_WIDTH = N_KV_HEADS * HEAD_DIM
ROPE_HALF = HEAD_DIM // 2
ROPE_THETA = 10000.0
CONV_WIDTH = D_MODEL
N_BRANCH = 2
FF_HIDDEN = 2816
EPS = 1e-6

_V7X_VMEM_BYTES = 64 * 1024 * 1024
_VMEM_LIMIT_BYTES = _V7X_VMEM_BYTES - 8 * 1024 * 1024
_HALO = 16

_PROJ_TM = 512
_PROJ_CK = 512
_ATTN_TQ = 256
_ATTN_TKV = 512
_POST_TM = 256
_FFN_CK = 256

_BF16 = jnp.bfloat16
_F32 = jnp.float32
_NT_DIMS = (((1,), (1,)), ((), ()))
_LOG2_E = 1.4426950408889634


def _dot(a, b):
    return jnp.dot(a, b, preferred_element_type=_F32)


def _dot_nt(a, b):
    return lax.dot_general(a, b, _NT_DIMS, preferred_element_type=_F32)


def _rms_rows(x, g):
    ms = jnp.mean(x * x, axis=-1, keepdims=True)
    return x * lax.rsqrt(ms + EPS) * g


def _resident(shape):
    nd = len(shape)
    return pl.BlockSpec(shape, lambda *_: (0,) * nd, pipeline_mode=pl.Buffered(1))


def _proj_kernel(xm_ref, xp_ref, xn_ref, gmix_ref, wcb_ref, wcc_ref, wcx_ref, wqt_ref, wk_ref,
                 wvt_ref, wg_ref, convw_ref, gq_ref, gk_ref, cost_ref, sint_ref, ck_ref, sk_ref,
                 yconv_ref, qt_ref, k_ref, vt_ref, g_ref, xn_scr, *, tiles_per_seq):
    tm = xm_ref.shape[0]
    i = pl.program_id(0)
    seq_first = (i % tiles_per_seq) == 0
    seq_last = (i % tiles_per_seq) == tiles_per_seq - 1
    gmix = gmix_ref[...]

    prev = _rms_rows(xp_ref[...], gmix)
    nxt = _rms_rows(xn_ref[...], gmix)
    xn_scr[0:_HALO, :] = jnp.where(seq_first, 0.0, prev).astype(_BF16)
    xn_scr[_HALO:_HALO + tm, :] = _rms_rows(xm_ref[...], gmix).astype(_BF16)
    xn_scr[_HALO + tm:, :] = jnp.where(seq_last, 0.0, nxt).astype(_BF16)

    xe = xn_scr[...]
    xm = xn_scr[_HALO:_HALO + tm, :]
    rows_ext = tm + 2 * _HALO

    for c0 in range(0, CONV_WIDTH, _PROJ_CK):
        cs = slice(c0, c0 + _PROJ_CK)
        u = _dot(xe, wcc_ref[:, cs]) * _dot(xe, wcx_ref[:, cs])
        u_prev = pltpu.roll(u, 1, axis=0)
        u_next = pltpu.roll(u, rows_ext - 1, axis=0)
        w = convw_ref[:, cs]
        conv = w[0:1, :] * u_prev + w[1:2, :] * u + w[2:3, :] * u_next
        cb = _dot(xm, wcb_ref[:, cs])
        yconv_ref[:, cs] = (cb * conv[_HALO:_HALO + tm, :]).astype(_BF16)

    cos_t = cost_ref[...]
    sin_t = sint_ref[...]
    gq = gq_ref[...]
    q_scale = HEAD_DIM ** -0.5 * _LOG2_E
    qt = _dot_nt(wqt_ref[...], xm)
    for h in range(N_HEADS):
        r0 = h * HEAD_DIM
        qh = qt[r0:r0 + HEAD_DIM, :]
        ms = jnp.mean(qh * qh, axis=0, keepdims=True)
        qh = qh * (lax.rsqrt(ms + EPS) * q_scale) * gq
        x0 = qh[:ROPE_HALF, :]
        x1 = qh[ROPE_HALF:, :]
        qt_ref[r0:r0 + ROPE_HALF, :] = (x0 * cos_t - x1 * sin_t).astype(_BF16)
        qt_ref[r0 + ROPE_HALF:r0 + HEAD_DIM, :] = (x0 * sin_t + x1 * cos_t).astype(_BF16)

    kf = _dot(xm, wk_ref[...])
    gk = gk_ref[...]
    ck = ck_ref[...]
    sk = sk_ref[...]
    for h in range(N_KV_HEADS):
        cs = slice(h * HEAD_DIM, (h + 1) * HEAD_DIM)
        kh = _rms_rows(kf[:, cs], gk)
        k_ref[:, cs] = (kh * ck + pltpu.roll(kh, ROPE_HALF, axis=1) * sk).astype(_BF16)

    vt_ref[...] = _dot_nt(wvt_ref[...], xm).astype(_BF16)

    for c0 in range(0, N_BRANCH * D_MODEL, _PROJ_CK):
        cs = slice(c0, c0 + _PROJ_CK)
        g_ref[:, cs] = jax.nn.sigmoid(_dot(xm, wg_ref[:, cs])).astype(_BF16)


def _proj_call(x2d, seq_len, gmix, wcb, wcc, wcx, wqt, wk, wvt, wg, convw, gq, gk,
               cos_t, sin_t, ck, sk):
    n_tok = x2d.shape[0]
    tm = _PROJ_TM
    assert seq_len % tm == 0 and tm % _HALO == 0
    tiles_per_seq = seq_len // tm
    halo_per_tile = tm // _HALO
    n_halo_blocks = n_tok // _HALO

    def prev_map(i):
        return (jnp.maximum(i * halo_per_tile - 1, 0), 0)

    def next_map(i):
        return (jnp.minimum((i + 1) * halo_per_tile, n_halo_blocks - 1), 0)

    in_specs = [
        pl.BlockSpec((tm, D_MODEL), lambda i: (i, 0)),
        pl.BlockSpec((_HALO, D_MODEL), prev_map),
        pl.BlockSpec((_HALO, D_MODEL), next_map),
        _resident(gmix.shape), _resident(wcb.shape), _resident(wcc.shape), _resident(wcx.shape),
        _resident(wqt.shape), _resident(wk.shape), _resident(wvt.shape), _resident(wg.shape),
        _resident(convw.shape), _resident(gq.shape), _resident(gk.shape),
        pl.BlockSpec((ROPE_HALF, tm), lambda i: (0, i % tiles_per_seq)),
        pl.BlockSpec((ROPE_HALF, tm), lambda i: (0, i % tiles_per_seq)),
        pl.BlockSpec((tm, HEAD_DIM), lambda i: (i % tiles_per_seq, 0)),
        pl.BlockSpec((tm, HEAD_DIM), lambda i: (i % tiles_per_seq, 0)),
    ]
    out_shape = (
        jax.ShapeDtypeStruct((n_tok, CONV_WIDTH), _BF16),
        jax.ShapeDtypeStruct((ATTN_WIDTH, n_tok), _BF16),
        jax.ShapeDtypeStruct((n_tok, KV_WIDTH), _BF16),
        jax.ShapeDtypeStruct((KV_WIDTH, n_tok), _BF16),
        jax.ShapeDtypeStruct((n_tok, N_BRANCH * D_MODEL), _BF16),
    )
    out_specs = (
        pl.BlockSpec((tm, CONV_WIDTH), lambda i: (i, 0)),
        pl.BlockSpec((ATTN_WIDTH, tm), lambda i: (0, i)),
        pl.BlockSpec((tm, KV_WIDTH), lambda i: (i, 0)),
        pl.BlockSpec((KV_WIDTH, tm), lambda i: (0, i)),
        pl.BlockSpec((tm, N_BRANCH * D_MODEL), lambda i: (i, 0)),
    )
    return pl.pallas_call(
        functools.partial(_proj_kernel, tiles_per_seq=tiles_per_seq),
        grid=(n_tok // tm,),
        in_specs=in_specs,
        out_specs=out_specs,
        out_shape=out_shape,
        scratch_shapes=[pltpu.VMEM((tm + 2 * _HALO, D_MODEL), _BF16)],
        compiler_params=pltpu.CompilerParams(
            dimension_semantics=("arbitrary",), vmem_limit_bytes=_VMEM_LIMIT_BYTES),
        name="proj",
    )(x2d, x2d, x2d, gmix, wcb, wcc, wcx, wqt, wk, wvt, wg, convw, gq, gk, cos_t, sin_t, ck, sk)


def _attn_kernel(qt_ref, k_ref, vt_ref, o_ref, acc_scr, s_even, s_odd):
    tq = qt_ref.shape[1]
    seq_len = k_ref.shape[0]
    tkv = _ATTN_TKV
    n_chunks = seq_len // tkv
    acc_scr[...] = jnp.zeros_like(acc_scr)

    def scores(c, s_ref):
        kc = k_ref[pl.ds(pl.multiple_of(c * tkv, tkv), tkv), :]
        for g in range(GROUP):
            s_ref[g] = _dot(kc, qt_ref[g * HEAD_DIM:(g + 1) * HEAD_DIM, :])

    def softmax_pv(c, s_ref, carry):
        vc = vt_ref[:, pl.ds(pl.multiple_of(c * tkv, tkv), tkv)]
        new = []
        for g in range(GROUP):
            m, l = carry[g]
            s = s_ref[g]
            m_new = jnp.maximum(m, jnp.max(s, axis=0, keepdims=True))
            p = jnp.exp2(s - m_new)
            alpha = jnp.exp2(m - m_new)
            l = alpha * l + jnp.sum(p, axis=0, keepdims=True)
            acc_scr[g] = alpha * acc_scr[g] + _dot(vc, p.astype(_BF16))
            new.append((m_new, l))
        return tuple(new)

    def body(j, carry):
        c = 2 * j
        scores(c + 1, s_odd)
        carry = softmax_pv(c, s_even, carry)
        scores(c + 2, s_even)
        return softmax_pv(c + 1, s_odd, carry)

    carry = tuple((jnp.full((1, tq), -1e30, _F32), jnp.zeros((1, tq), _F32))
                  for _ in range(GROUP))
    scores(0, s_even)
    carry = lax.fori_loop(0, n_chunks // 2 - 1, body, carry)
    scores(n_chunks - 1, s_odd)
    carry = softmax_pv(n_chunks - 2, s_even, carry)
    carry = softmax_pv(n_chunks - 1, s_odd, carry)
    for g in range(GROUP):
        o_ref[:, g * HEAD_DIM:(g + 1) * HEAD_DIM] = (acc_scr[g] / carry[g][1]).T.astype(_BF16)


def _attn_call(qt, k, vt, batch, seq_len):
    n_tok = batch * seq_len
    tq = _ATTN_TQ
    assert seq_len % tq == 0 and seq_len % (2 * _ATTN_TKV) == 0
    q_tiles = seq_len // tq
    group_w = GROUP * HEAD_DIM
    return pl.pallas_call(
        _attn_kernel,
        grid=(batch, N_KV_HEADS, q_tiles),
        in_specs=[
            pl.BlockSpec((group_w, tq), lambda b, h, i: (h, b * q_tiles + i)),
            pl.BlockSpec((seq_len, HEAD_DIM), lambda b, h, i: (b, h)),
            pl.BlockSpec((HEAD_DIM, seq_len), lambda b, h, i: (h, b)),
        ],
        out_specs=pl.BlockSpec((tq, group_w), lambda b, h, i: (b * q_tiles + i, h)),
        out_shape=jax.ShapeDtypeStruct((n_tok, ATTN_WIDTH), _BF16),
        scratch_shapes=[pltpu.VMEM((GROUP, HEAD_DIM, tq), _F32),
                        pltpu.VMEM((GROUP, _ATTN_TKV, tq), _F32),
                        pltpu.VMEM((GROUP, _ATTN_TKV, tq), _F32)],
        compiler_params=pltpu.CompilerParams(
            dimension_semantics=("arbitrary", "arbitrary", "arbitrary"),
            vmem_limit_bytes=_VMEM_LIMIT_BYTES),
        name="attn",
    )(qt, k, vt)


def _post_kernel(x_ref, yc_ref, ya_ref, g_ref, wb0_ref, wb1_ref, wout_ref, gffn_ref, wfg_ref,
                 wfu_ref, wfo_ref, gfin_ref, o_ref, act_scr, *, final_norm):
    g = g_ref[...].astype(_F32)
    merged = g[:, :D_MODEL] * _dot(yc_ref[...], wb0_ref[...])
    merged += g[:, D_MODEL:] * _dot(ya_ref[...], wb1_ref[...])
    x1 = x_ref[...] + _dot(merged.astype(_BF16), wout_ref[...])

    xn = _rms_rows(x1, gffn_ref[...]).astype(_BF16)
    for c0 in range(0, FF_HIDDEN, _FFN_CK):
        cs = slice(c0, c0 + _FFN_CK)
        gate = _dot(xn, wfg_ref[:, cs])
        up = _dot(xn, wfu_ref[:, cs])
        act_scr[:, cs] = (gate * jax.nn.sigmoid(gate) * up).astype(_BF16)
    x2 = x1 + _dot(act_scr[...], wfo_ref[...])
    if final_norm:
        x2 = _rms_rows(x2, gfin_ref[...])
    o_ref[...] = x2


def _post_call(x2d, yconv, yattn, gates, wb0, wb1, wout, gffn, wfg, wfu, wfo, gfin, final_norm):
    n_tok = x2d.shape[0]
    tm = _POST_TM
    assert n_tok % tm == 0 and FF_HIDDEN % _FFN_CK == 0

    def row_spec(width):
        return pl.BlockSpec((tm, width), lambda i: (i, 0))

    return pl.pallas_call(
        functools.partial(_post_kernel, final_norm=final_norm),
        grid=(n_tok // tm,),
        in_specs=[
            row_spec(D_MODEL), row_spec(CONV_WIDTH), row_spec(ATTN_WIDTH),
            row_spec(N_BRANCH * D_MODEL),
            _resident(wb0.shape), _resident(wb1.shape), _resident(wout.shape),
            _resident(gffn.shape), _resident(wfg.shape), _resident(wfu.shape),
            _resident(wfo.shape), _resident(gfin.shape),
        ],
        out_specs=row_spec(D_MODEL),
        out_shape=jax.ShapeDtypeStruct((n_tok, D_MODEL), _F32),
        scratch_shapes=[pltpu.VMEM((tm, FF_HIDDEN), _BF16)],
        compiler_params=pltpu.CompilerParams(
            dimension_semantics=("arbitrary",), vmem_limit_bytes=_VMEM_LIMIT_BYTES),
        name="post",
    )(x2d, yconv, yattn, gates, wb0, wb1, wout, gffn, wfg, wfu, wfo, gfin)


_HEAD_PERM = np.concatenate([np.arange(0, HEAD_DIM, 2), np.arange(1, HEAD_DIM, 2)])


def _permute_heads(w, n_heads):
    return w.reshape(w.shape[0], n_heads, HEAD_DIM)[:, :, _HEAD_PERM].reshape(w.shape[0], -1)


def _layer_weights(w_in, conv_w, q_norm_g, k_norm_g, w_branch, w_out, w_ffn_in, w_ffn_out):
    o = np.cumsum([0, CONV_WIDTH, CONV_WIDTH, CONV_WIDTH, ATTN_WIDTH, KV_WIDTH, KV_WIDTH])
    wb = w_in.astype(_BF16)
    return dict(
        wcb=wb[:, o[0]:o[1]], wcc=wb[:, o[1]:o[2]], wcx=wb[:, o[2]:o[3]],
        wqt=_permute_heads(wb[:, o[3]:o[4]], N_HEADS).T,
        wk=_permute_heads(wb[:, o[4]:o[5]], N_KV_HEADS),
        wvt=wb[:, o[5]:o[6]].T,
        wg=wb[:, o[6]:],
        convw=conv_w,
        gq=q_norm_g[_HEAD_PERM].reshape(HEAD_DIM, 1),
        gk=k_norm_g[_HEAD_PERM].reshape(1, HEAD_DIM),
        wb0=w_branch[0].astype(_BF16), wb1=w_branch[1].astype(_BF16),
        wout=w_out.astype(_BF16),
        wfg=w_ffn_in[:, :FF_HIDDEN].astype(_BF16), wfu=w_ffn_in[:, FF_HIDDEN:].astype(_BF16),
        wfo=w_ffn_out.astype(_BF16),
    )


def _rope_tables(seq_len):
    n_rows = seq_len // GRID_W
    rows = jnp.repeat(jnp.arange(n_rows, dtype=_F32), GRID_W)
    cols = jnp.tile(jnp.arange(GRID_W, dtype=_F32), n_rows)
    inv_freq = ROPE_THETA ** (-jnp.arange(0, ROPE_HALF, 2, dtype=_F32) / ROPE_HALF)
    ang = jnp.concatenate([rows[:, None] * inv_freq[None, :], cols[:, None] * inv_freq[None, :]],
                          axis=-1)
    c, s = jnp.cos(ang), jnp.sin(ang)
    return dict(cos_t=c.T, sin_t=s.T,
                ck=jnp.concatenate([c, c], axis=-1), sk=jnp.concatenate([-s, s], axis=-1))


def _trunk(x, layers, norm_mix_g, norm_ffn_g, norm_final_g):
    batch, seq_len, _ = x.shape
    x2d = x.reshape(batch * seq_len, D_MODEL)
    rope = _rope_tables(seq_len)
    gfin = norm_final_g.reshape(1, D_MODEL)
    for l, w in enumerate(layers):
        yconv, qt, k, vt, gates = _proj_call(
            x2d, seq_len, norm_mix_g[l].reshape(1, D_MODEL), w["wcb"], w["wcc"], w["wcx"],
            w["wqt"], w["wk"], w["wvt"], w["wg"], w["convw"], w["gq"], w["gk"],
            rope["cos_t"], rope["sin_t"], rope["ck"], rope["sk"])
        yattn = _attn_call(qt, k, vt, batch, seq_len)
        x2d = _post_call(x2d, yconv, yattn, gates, w["wb0"], w["wb1"], w["wout"],
                         norm_ffn_g[l].reshape(1, D_MODEL), w["wfg"], w["wfu"], w["wfo"], gfin,
                         final_norm=(l == len(layers) - 1))
    return x2d.reshape(batch, seq_len, D_MODEL)


def kernel(x_prompt, x_sample, norm_mix_g, w_in, conv_w, q_norm_g, k_norm_g, w_branch, w_out,
           norm_ffn_g, w_ffn_in, w_ffn_out, norm_final_g):
    depth = w_in.shape[0]
    layers = [_layer_weights(w_in[l], conv_w[l], q_norm_g[l], k_norm_g[l], w_branch[l], w_out[l],
                             w_ffn_in[l], w_ffn_out[l]) for l in range(depth)]
    y_prompt = _trunk(x_prompt, layers, norm_mix_g, norm_ffn_g, norm_final_g)
    y_sample = _trunk(x_sample, layers, norm_mix_g, norm_ffn_g, norm_final_g)
    return (y_prompt, y_sample)
```

```python
import functools

import jax
import jax.numpy as jnp
import numpy as np
from jax import lax
from jax.experimental import pallas as pl
from jax.experimental.pallas import tpu as pltpu

D_MODEL = 1024
GRID_W = 64
N_HEADS = 8
N_KV_HEADS = 2
HEAD_DIM = 128
GROUP = N_HEADS // N_KV_HEADS
ATTN_WIDTH = N_HEADS * HEAD_DIM
KV_WIDTH = N_KV_HEADS * HEAD_DIM
ROPE_HALF = HEAD_DIM // 2
ROPE_THETA = 10000.0
CONV_WIDTH = D_MODEL
N_BRANCH = 2
FF_HIDDEN = 2816
EPS = 1e-6

_V7X_VMEM_BYTES = 64 * 1024 * 1024
_VMEM_LIMIT_BYTES = _V7X_VMEM_BYTES - 8 * 1024 * 1024
_HALO = 16

_PROJ_TM = 512
_PROJ_CK = 512
_ATTN_TQ = 512
_ATTN_TKV = 512
_POST_TM = 256
_FFN_CK = 256

_BF16 = jnp.bfloat16
_F32 = jnp.float32
_NT_DIMS = (((1,), (1,)), ((), ()))
_LOG2_E = 1.4426950408889634


def _dot(a, b):
    return jnp.dot(a, b, preferred_element_type=_F32)


def _dot_nt(a, b):
    return lax.dot_general(a, b, _NT_DIMS, preferred_element_type=_F32)


def _rms_rows(x, g):
    ms = jnp.mean(x * x, axis=-1, keepdims=True)
    return x * lax.rsqrt(ms + EPS) * g


def _resident(shape):
    nd = len(shape)
    return pl.BlockSpec(shape, lambda *_: (0,) * nd, pipeline_mode=pl.Buffered(1))


def _proj_kernel(xm_ref, xp_ref, xn_ref, gmix_ref, wcb_ref, wcc_ref, wcx_ref, wqt_ref, wk_ref,
                 wvt_ref, wg_ref, convw_ref, gq_ref, gk_ref, cost_ref, sint_ref, ck_ref, sk_ref,
                 yconv_ref, qt_ref, k_ref, vt_ref, g_ref, xn_scr, *, tiles_per_seq):
    tm = xm_ref.shape[0]
    i = pl.program_id(0)
    seq_first = (i % tiles_per_seq) == 0
    seq_last = (i % tiles_per_seq) == tiles_per_seq - 1
    gmix = gmix_ref[...]

    prev = _rms_rows(xp_ref[...], gmix)
    nxt = _rms_rows(xn_ref[...], gmix)
    xn_scr[0:_HALO, :] = jnp.where(seq_first, 0.0, prev).astype(_BF16)
    xn_scr[_HALO:_HALO + tm, :] = _rms_rows(xm_ref[...], gmix).astype(_BF16)
    xn_scr[_HALO + tm:, :] = jnp.where(seq_last, 0.0, nxt).astype(_BF16)

    xe = xn_scr[...]
    xm = xn_scr[_HALO:_HALO + tm, :]
    rows_ext = tm + 2 * _HALO

    for c0 in range(0, CONV_WIDTH, _PROJ_CK):
        cs = slice(c0, c0 + _PROJ_CK)
        u = _dot(xe, wcc_ref[:, cs]) * _dot(xe, wcx_ref[:, cs])
        u_prev = pltpu.roll(u, 1, axis=0)
        u_next = pltpu.roll(u, rows_ext - 1, axis=0)
        w = convw_ref[:, cs]
        conv = w[0:1, :] * u_prev + w[1:2, :] * u + w[2:3, :] * u_next
        cb = _dot(xm, wcb_ref[:, cs])
        yconv_ref[:, cs] = (cb * conv[_HALO:_HALO + tm, :]).astype(_BF16)

    cos_t = cost_ref[...]
    sin_t = sint_ref[...]
    gq = gq_ref[...]
    q_scale = HEAD_DIM ** -0.5 * _LOG2_E
    qt = _dot_nt(wqt_ref[...], xm)
    for h in range(N_HEADS):
        r0 = h * HEAD_DIM
        qh = qt[r0:r0 + HEAD_DIM, :]
        ms = jnp.mean(qh * qh, axis=0, keepdims=True)
        qh = qh * (lax.rsqrt(ms + EPS) * q_scale) * gq
        x0 = qh[:ROPE_HALF, :]
        x1 = qh[ROPE_HALF:, :]
        qt_ref[r0:r0 + ROPE_HALF, :] = (x0 * cos_t - x1 * sin_t).astype(_BF16)
        qt_ref[r0 + ROPE_HALF:r0 + HEAD_DIM, :] = (x0 * sin_t + x1 * cos_t).astype(_BF16)

    kf = _dot(xm, wk_ref[...])
    gk = gk_ref[...]
    ck = ck_ref[...]
    sk = sk_ref[...]
    for h in range(N_KV_HEADS):
        cs = slice(h * HEAD_DIM, (h + 1) * HEAD_DIM)
        kh = _rms_rows(kf[:, cs], gk)
        k_ref[:, cs] = (kh * ck + pltpu.roll(kh, ROPE_HALF, axis=1) * sk).astype(_BF16)

    vt_ref[...] = _dot_nt(wvt_ref[...], xm).astype(_BF16)

    for c0 in range(0, N_BRANCH * D_MODEL, _PROJ_CK):
        cs = slice(c0, c0 + _PROJ_CK)
        g_ref[:, cs] = jax.nn.sigmoid(_dot(xm, wg_ref[:, cs])).astype(_BF16)


def _proj_call(x2d, seq_len, gmix, wcb, wcc, wcx, wqt, wk, wvt, wg, convw, gq, gk,
               cos_t, sin_t, ck, sk):
    n_tok = x2d.shape[0]
    tm = _PROJ_TM
    assert seq_len % tm == 0 and tm % _HALO == 0
    tiles_per_seq = seq_len // tm
    halo_per_tile = tm // _HALO
    n_halo_blocks = n_tok // _HALO

    def prev_map(i):
        return (jnp.maximum(i * halo_per_tile - 1, 0), 0)

    def next_map(i):
        return (jnp.minimum((i + 1) * halo_per_tile, n_halo_blocks - 1), 0)

    in_specs = [
        pl.BlockSpec((tm, D_MODEL), lambda i: (i, 0)),
        pl.BlockSpec((_HALO, D_MODEL), prev_map),
        pl.BlockSpec((_HALO, D_MODEL), next_map),
        _resident(gmix.shape), _resident(wcb.shape), _resident(wcc.shape), _resident(wcx.shape),
        _resident(wqt.shape), _resident(wk.shape), _resident(wvt.shape), _resident(wg.shape),
        _resident(convw.shape), _resident(gq.shape), _resident(gk.shape),
        pl.BlockSpec((ROPE_HALF, tm), lambda i: (0, i % tiles_per_seq)),
        pl.BlockSpec((ROPE_HALF, tm), lambda i: (0, i % tiles_per_seq)),
        pl.BlockSpec((tm, HEAD_DIM), lambda i: (i % tiles_per_seq, 0)),
        pl.BlockSpec((tm, HEAD_DIM), lambda i: (i % tiles_per_seq, 0)),
    ]
    out_shape = (
        jax.ShapeDtypeStruct((n_tok, CONV_WIDTH), _BF16),
        jax.ShapeDtypeStruct((ATTN_WIDTH, n_tok), _BF16),
        jax.ShapeDtypeStruct((n_tok, KV_WIDTH), _BF16),
        jax.ShapeDtypeStruct((KV_WIDTH, n_tok), _BF16),
        jax.ShapeDtypeStruct((n_tok, N_BRANCH * D_MODEL), _BF16),
    )
    out_specs = (
        pl.BlockSpec((tm, CONV_WIDTH), lambda i: (i, 0)),
        pl.BlockSpec((ATTN_WIDTH, tm), lambda i: (0, i)),
        pl.BlockSpec((tm, KV_WIDTH), lambda i: (i, 0)),
        pl.BlockSpec((KV_WIDTH, tm), lambda i: (0, i)),
        pl.BlockSpec((tm, N_BRANCH * D_MODEL), lambda i: (i, 0)),
    )
    return pl.pallas_call(
        functools.partial(_proj_kernel, tiles_per_seq=tiles_per_seq),
        grid=(n_tok // tm,),
        in_specs=in_specs,
        out_specs=out_specs,
        out_shape=out_shape,
        scratch_shapes=[pltpu.VMEM((tm + 2 * _HALO, D_MODEL), _BF16)],
        compiler_params=pltpu.CompilerParams(
            dimension_semantics=("arbitrary",), vmem_limit_bytes=_VMEM_LIMIT_BYTES),
        name="proj",
    )(x2d, x2d, x2d, gmix, wcb, wcc, wcx, wqt, wk, wvt, wg, convw, gq, gk, cos_t, sin_t, ck, sk)


def _attn_kernel(qt_ref, k_ref, vt_ref, o_ref, acc_scr, s_even, s_odd, mx_even, mx_odd):
    tq = qt_ref.shape[1]
    seq_len = k_ref.shape[0]
    tkv = _ATTN_TKV
    n_chunks = seq_len // tkv
    acc_scr[...] = jnp.zeros_like(acc_scr)

    def kv_slice(c):
        return pl.ds(pl.multiple_of(c * tkv, tkv), tkv)

    def scores(g, kc, s_ref, mx_ref):
        s = _dot(kc, qt_ref[g * HEAD_DIM:(g + 1) * HEAD_DIM, :])
        s_ref[g] = s
        mx_ref[g] = jnp.max(s, axis=0, keepdims=True)

    def softmax_pv(g, vc, s_ref, mx_ref, m, l):
        m_new = jnp.maximum(m, mx_ref[g])
        p = jnp.exp2(s_ref[g] - m_new)
        alpha = jnp.exp2(m - m_new)
        l = alpha * l + jnp.sum(p, axis=0, keepdims=True)
        acc_scr[g] = alpha * acc_scr[g] + _dot(vc, p.astype(_BF16))
        return m_new, l

    def stage(c_next, s_next, mx_next, c_cur, s_cur, mx_cur, state):
        kc = None if c_next is None else k_ref[kv_slice(c_next), :]
        vc = None if c_cur is None else vt_ref[:, kv_slice(c_cur)]
        new = []
        for g in range(GROUP):
            if c_next is not None:
                scores(g, kc, s_next, mx_next)
            if c_cur is not None:
                new.append(softmax_pv(g, vc, s_cur, mx_cur, *state[g]))
        return tuple(new) if new else state

    def body(j, state):
        c = 2 * j
        state = stage(c + 1, s_odd, mx_odd, c, s_even, mx_even, state)
        return stage(c + 2, s_even, mx_even, c + 1, s_odd, mx_odd, state)

    state = tuple((jnp.full((1, tq), -1e30, _F32), jnp.zeros((1, tq), _F32))
                  for _ in range(GROUP))
    state = stage(0, s_even, mx_even, None, None, None, state)
    state = lax.fori_loop(0, n_chunks // 2 - 1, body, state)
    state = stage(n_chunks - 1, s_odd, mx_odd, n_chunks - 2, s_even, mx_even, state)
    state = stage(None, None, None, n_chunks - 1, s_odd, mx_odd, state)
    for g in range(GROUP):
        o_ref[:, g * HEAD_DIM:(g + 1) * HEAD_DIM] = (acc_scr[g] / state[g][1]).T.astype(_BF16)


def _attn_call(qt, k, vt, batch, seq_len):
    n_tok = batch * seq_len
    tq = _ATTN_TQ
    assert seq_len % tq == 0 and seq_len % (2 * _ATTN_TKV) == 0
    q_tiles = seq_len // tq
    group_w = GROUP * HEAD_DIM
    return pl.pallas_call(
        _attn_kernel,
        grid=(batch, N_KV_HEADS, q_tiles),
        in_specs=[
            pl.BlockSpec((group_w, tq), lambda b, h, i: (h, b * q_tiles + i)),
            pl.BlockSpec((seq_len, HEAD_DIM), lambda b, h, i: (b, h)),
            pl.BlockSpec((HEAD_DIM, seq_len), lambda b, h, i: (h, b)),
        ],
        out_specs=pl.BlockSpec((tq, group_w), lambda b, h, i: (b * q_tiles + i, h)),
        out_shape=jax.ShapeDtypeStruct((n_tok, ATTN_WIDTH), _BF16),
        scratch_shapes=[pltpu.VMEM((GROUP, HEAD_DIM, tq), _F32),
                        pltpu.VMEM((GROUP, _ATTN_TKV, tq), _F32),
                        pltpu.VMEM((GROUP, _ATTN_TKV, tq), _F32),
                        pltpu.VMEM((GROUP, 1, tq), _F32),
                        pltpu.VMEM((GROUP, 1, tq), _F32)],
        compiler_params=pltpu.CompilerParams(
            dimension_semantics=("arbitrary", "arbitrary", "arbitrary"),
            vmem_limit_bytes=_VMEM_LIMIT_BYTES),
        name="attn",
    )(qt, k, vt)


def _post_kernel(x_ref, yc_ref, ya_ref, g_ref, wb0_ref, wb1_ref, wout_ref, gffn_ref, wfg_ref,
                 wfu_ref, wfo_ref, gfin_ref, o_ref, act_scr, *, final_norm):
    g = g_ref[...].astype(_F32)
    merged = g[:, :D_MODEL] * _dot(yc_ref[...], wb0_ref[...])
    merged += g[:, D_MODEL:] * _dot(ya_ref[...], wb1_ref[...])
    x1 = x_ref[...] + _dot(merged.astype(_BF16), wout_ref[...])

    xn = _rms_rows(x1, gffn_ref[...]).astype(_BF16)
    for c0 in range(0, FF_HIDDEN, _FFN_CK):
        cs = slice(c0, c0 + _FFN_CK)
        gate = _dot(xn, wfg_ref[:, cs])
        up = _dot(xn, wfu_ref[:, cs])
        act_scr[:, cs] = (gate * jax.nn.sigmoid(gate) * up).astype(_BF16)
    x2 = x1 + _dot(act_scr[...], wfo_ref[...])
    if final_norm:
        x2 = _rms_rows(x2, gfin_ref[...])
    o_ref[...] = x2


def _post_call(x2d, yconv, yattn, gates, wb0, wb1, wout, gffn, wfg, wfu, wfo, gfin, final_norm):
    n_tok = x2d.shape[0]
    tm = _POST_TM
    assert n_tok % tm == 0 and FF_HIDDEN % _FFN_CK == 0

    def row_spec(width):
        return pl.BlockSpec((tm, width), lambda i: (i, 0))

    return pl.pallas_call(
        functools.partial(_post_kernel, final_norm=final_norm),
        grid=(n_tok // tm,),
        in_specs=[
            row_spec(D_MODEL), row_spec(CONV_WIDTH), row_spec(ATTN_WIDTH),
            row_spec(N_BRANCH * D_MODEL),
            _resident(wb0.shape), _resident(wb1.shape), _resident(wout.shape),
            _resident(gffn.shape), _resident(wfg.shape), _resident(wfu.shape),
            _resident(wfo.shape), _resident(gfin.shape),
        ],
        out_specs=row_spec(D_MODEL),
        out_shape=jax.ShapeDtypeStruct((n_tok, D_MODEL), _F32),
        scratch_shapes=[pltpu.VMEM((tm, FF_HIDDEN), _BF16)],
        compiler_params=pltpu.CompilerParams(
            dimension_semantics=("arbitrary",), vmem_limit_bytes=_VMEM_LIMIT_BYTES),
        name="post",
    )(x2d, yconv, yattn, gates, wb0, wb1, wout, gffn, wfg, wfu, wfo, gfin)


_HEAD_PERM = np.concatenate([np.arange(0, HEAD_DIM, 2), np.arange(1, HEAD_DIM, 2)])


def _permute_heads(w, n_heads):
    return w.reshape(w.shape[0], n_heads, HEAD_DIM)[:, :, _HEAD_PERM].reshape(w.shape[0], -1)


def _layer_weights(w_in, conv_w, q_norm_g, k_norm_g, w_branch, w_out, w_ffn_in, w_ffn_out):
    o = np.cumsum([0, CONV_WIDTH, CONV_WIDTH, CONV_WIDTH, ATTN_WIDTH, KV_WIDTH, KV_WIDTH])
    wb = w_in.astype(_BF16)
    return dict(
        wcb=wb[:, o[0]:o[1]], wcc=wb[:, o[1]:o[2]], wcx=wb[:, o[2]:o[3]],
        wqt=_permute_heads(wb[:, o[3]:o[4]], N_HEADS).T,
        wk=_permute_heads(wb[:, o[4]:o[5]], N_KV_HEADS),
        wvt=wb[:, o[5]:o[6]].T,
        wg=wb[:, o[6]:],
        convw=conv_w,
        gq=q_norm_g[_HEAD_PERM].reshape(HEAD_DIM, 1),
        gk=k_norm_g[_HEAD_PERM].reshape(1, HEAD_DIM),
        wb0=w_branch[0].astype(_BF16), wb1=w_branch[1].astype(_BF16),
        wout=w_out.astype(_BF16),
        wfg=w_ffn_in[:, :FF_HIDDEN].astype(_BF16), wfu=w_ffn_in[:, FF_HIDDEN:].astype(_BF16),
        wfo=w_ffn_out.astype(_BF16),
    )


def _rope_tables(seq_len):
    n_rows = seq_len // GRID_W
    rows = jnp.repeat(jnp.arange(n_rows, dtype=_F32), GRID_W)
    cols = jnp.tile(jnp.arange(GRID_W, dtype=_F32), n_rows)
    inv_freq = ROPE_THETA ** (-jnp.arange(0, ROPE_HALF, 2, dtype=_F32) / ROPE_HALF)
    ang = jnp.concatenate([rows[:, None] * inv_freq[None, :], cols[:, None] * inv_freq[None, :]],
                          axis=-1)
    c, s = jnp.cos(ang), jnp.sin(ang)
    return dict(cos_t=c.T, sin_t=s.T,
                ck=jnp.concatenate([c, c], axis=-1), sk=jnp.concatenate([-s, s], axis=-1))


def _trunk(x, layers, norm_mix_g, norm_ffn_g, norm_final_g):
    batch, seq_len, _ = x.shape
    x2d = x.reshape(batch * seq_len, D_MODEL)
    rope = _rope_tables(seq_len)
    gfin = norm_final_g.reshape(1, D_MODEL)
    for l, w in enumerate(layers):
        yconv, qt, k, vt, gates = _proj_call(
            x2d, seq_len, norm_mix_g[l].reshape(1, D_MODEL), w["wcb"], w["wcc"], w["wcx"],
            w["wqt"], w["wk"], w["wvt"], w["wg"], w["convw"], w["gq"], w["gk"],
            rope["cos_t"], rope["sin_t"], rope["ck"], rope["sk"])
        yattn = _attn_call(qt, k, vt, batch, seq_len)
        x2d = _post_call(x2d, yconv, yattn, gates, w["wb0"], w["wb1"], w["wout"],
                         norm_ffn_g[l].reshape(1, D_MODEL), w["wfg"], w["wfu"], w["wfo"], gfin,
                         final_norm=(l == len(layers) - 1))
    return x2d.reshape(batch, seq_len, D_MODEL)


def kernel(x_prompt, x_sample, norm_mix_g, w_in, conv_w, q_norm_g, k_norm_g, w_branch, w_out,
           norm_ffn_g, w_ffn_in, w_ffn_out, norm_final_g):
    depth = w_in.shape[0]
    layers = [_layer_weights(w_in[l], conv_w[l], q_norm_g[l], k_norm_g[l], w_branch[l], w_out[l],
                             w_ffn_in[l], w_ffn_out[l]) for l in range(depth)]
    y_prompt = _trunk(x_prompt, layers, norm_mix_g, norm_ffn_g, norm_final_g)
    y_sample = _trunk(x_sample, layers, norm_mix_g, norm_ffn_g, norm_final_g)
    return (y_prompt, y_sample)
```

```python
import functools

import jax
import jax.numpy as jnp
import numpy as np
from jax import lax
from jax.experimental import pallas as pl
from jax.experimental.pallas import tpu as pltpu

D_MODEL = 1024
GRID_W = 64
N_HEADS = 8
N_KV_HEADS = 2
HEAD_DIM = 128
GROUP = N_HEADS // N_KV_HEADS
ATTN_WIDTH = N_HEADS * HEAD_DIM
KV_WIDTH = N_KV_HEADS * HEAD_DIM
ROPE_HALF = HEAD_DIM // 2
ROPE_THETA = 10000.0
CONV_WIDTH = D_MODEL
N_BRANCH = 2
FF_HIDDEN = 2816
EPS = 1e-6

_V7X_VMEM_BYTES = 64 * 1024 * 1024
_VMEM_LIMIT_BYTES = _V7X_VMEM_BYTES - 8 * 1024 * 1024
_HALO = 16

_PROJ_TM = 512
_PROJ_CK = 512
_ATTN_TQ = 512
_ATTN_TKV = 512
_POST_TM = 512
_POST_GROUPS = 2
_FFN_CK = 256

_BF16 = jnp.bfloat16
_F32 = jnp.float32
_NT_DIMS = (((1,), (1,)), ((), ()))
_LOG2_E = 1.4426950408889634


def _dot(a, b):
    return jnp.dot(a, b, preferred_element_type=_F32)


def _dot_nt(a, b):
    return lax.dot_general(a, b, _NT_DIMS, preferred_element_type=_F32)


def _rms_rows(x, g):
    ms = jnp.mean(x * x, axis=-1, keepdims=True)
    return x * lax.rsqrt(ms + EPS) * g


def _resident(shape):
    nd = len(shape)
    return pl.BlockSpec(shape, lambda *_: (0,) * nd, pipeline_mode=pl.Buffered(1))


def _proj_kernel(xm_ref, xp_ref, xn_ref, gmix_ref, wcb_ref, wcc_ref, wcx_ref, wqt_ref, wk_ref,
                 wvt_ref, wg_ref, convw_ref, gq_ref, gk_ref, cost_ref, sint_ref, ck_ref, sk_ref,
                 yconv_ref, qt_ref, k_ref, vt_ref, g_ref, xn_scr, *, tiles_per_seq):
    tm = xm_ref.shape[0]
    i = pl.program_id(0)
    seq_first = (i % tiles_per_seq) == 0
    seq_last = (i % tiles_per_seq) == tiles_per_seq - 1
    gmix = gmix_ref[...]

    prev = _rms_rows(xp_ref[...], gmix)
    nxt = _rms_rows(xn_ref[...], gmix)
    xn_scr[0:_HALO, :] = jnp.where(seq_first, 0.0, prev).astype(_BF16)
    xn_scr[_HALO:_HALO + tm, :] = _rms_rows(xm_ref[...], gmix).astype(_BF16)
    xn_scr[_HALO + tm:, :] = jnp.where(seq_last, 0.0, nxt).astype(_BF16)

    xe = xn_scr[...]
    xm = xn_scr[_HALO:_HALO + tm, :]
    rows_ext = tm + 2 * _HALO

    for c0 in range(0, CONV_WIDTH, _PROJ_CK):
        cs = slice(c0, c0 + _PROJ_CK)
        u = _dot(xe, wcc_ref[:, cs]) * _dot(xe, wcx_ref[:, cs])
        u_prev = pltpu.roll(u, 1, axis=0)
        u_next = pltpu.roll(u, rows_ext - 1, axis=0)
        w = convw_ref[:, cs]
        conv = w[0:1, :] * u_prev + w[1:2, :] * u + w[2:3, :] * u_next
        cb = _dot(xm, wcb_ref[:, cs])
        yconv_ref[:, cs] = (cb * conv[_HALO:_HALO + tm, :]).astype(_BF16)

    cos_t = cost_ref[...]
    sin_t = sint_ref[...]
    gq = gq_ref[...]
    q_scale = HEAD_DIM ** -0.5 * _LOG2_E
    qt = _dot_nt(wqt_ref[...], xm)
    for h in range(N_HEADS):
        r0 = h * HEAD_DIM
        qh = qt[r0:r0 + HEAD_DIM, :]
        ms = jnp.mean(qh * qh, axis=0, keepdims=True)
        qh = qh * (lax.rsqrt(ms + EPS) * q_scale) * gq
        x0 = qh[:ROPE_HALF, :]
        x1 = qh[ROPE_HALF:, :]
        qt_ref[r0:r0 + ROPE_HALF, :] = (x0 * cos_t - x1 * sin_t).astype(_BF16)
        qt_ref[r0 + ROPE_HALF:r0 + HEAD_DIM, :] = (x0 * sin_t + x1 * cos_t).astype(_BF16)

    kf = _dot(xm, wk_ref[...])
    gk = gk_ref[...]
    ck = ck_ref[...]
    sk = sk_ref[...]
    for h in range(N_KV_HEADS):
        cs = slice(h * HEAD_DIM, (h + 1) * HEAD_DIM)
        kh = _rms_rows(kf[:, cs], gk)
        k_ref[:, cs] = (kh * ck + pltpu.roll(kh, ROPE_HALF, axis=1) * sk).astype(_BF16)

    vt_ref[...] = _dot_nt(wvt_ref[...], xm).astype(_BF16)

    for c0 in range(0, N_BRANCH * D_MODEL, _PROJ_CK):
        cs = slice(c0, c0 + _PROJ_CK)
        g_ref[:, cs] = jax.nn.sigmoid(_dot(xm, wg_ref[:, cs])).astype(_BF16)


def _proj_call(x2d, seq_len, gmix, wcb, wcc, wcx, wqt, wk, wvt, wg, convw, gq, gk,
               cos_t, sin_t, ck, sk):
    n_tok = x2d.shape[0]
    tm = _PROJ_TM
    assert seq_len % tm == 0 and tm % _HALO == 0
    tiles_per_seq = seq_len // tm
    halo_per_tile = tm // _HALO
    n_halo_blocks = n_tok // _HALO

    def prev_map(i):
        return (jnp.maximum(i * halo_per_tile - 1, 0), 0)

    def next_map(i):
        return (jnp.minimum((i + 1) * halo_per_tile, n_halo_blocks - 1), 0)

    in_specs = [
        pl.BlockSpec((tm, D_MODEL), lambda i: (i, 0)),
        pl.BlockSpec((_HALO, D_MODEL), prev_map),
        pl.BlockSpec((_HALO, D_MODEL), next_map),
        _resident(gmix.shape), _resident(wcb.shape), _resident(wcc.shape), _resident(wcx.shape),
        _resident(wqt.shape), _resident(wk.shape), _resident(wvt.shape), _resident(wg.shape),
        _resident(convw.shape), _resident(gq.shape), _resident(gk.shape),
        pl.BlockSpec((ROPE_HALF, tm), lambda i: (0, i % tiles_per_seq)),
        pl.BlockSpec((ROPE_HALF, tm), lambda i: (0, i % tiles_per_seq)),
        pl.BlockSpec((tm, HEAD_DIM), lambda i: (i % tiles_per_seq, 0)),
        pl.BlockSpec((tm, HEAD_DIM), lambda i: (i % tiles_per_seq, 0)),
    ]
    out_shape = (
        jax.ShapeDtypeStruct((n_tok, CONV_WIDTH), _BF16),
        jax.ShapeDtypeStruct((ATTN_WIDTH, n_tok), _BF16),
        jax.ShapeDtypeStruct((n_tok, KV_WIDTH), _BF16),
        jax.ShapeDtypeStruct((KV_WIDTH, n_tok), _BF16),
        jax.ShapeDtypeStruct((n_tok, N_BRANCH * D_MODEL), _BF16),
    )
    out_specs = (
        pl.BlockSpec((tm, CONV_WIDTH), lambda i: (i, 0)),
        pl.BlockSpec((ATTN_WIDTH, tm), lambda i: (0, i)),
        pl.BlockSpec((tm, KV_WIDTH), lambda i: (i, 0)),
        pl.BlockSpec((KV_WIDTH, tm), lambda i: (0, i)),
        pl.BlockSpec((tm, N_BRANCH * D_MODEL), lambda i: (i, 0)),
    )
    return pl.pallas_call(
        functools.partial(_proj_kernel, tiles_per_seq=tiles_per_seq),
        grid=(n_tok // tm,),
        in_specs=in_specs,
        out_specs=out_specs,
        out_shape=out_shape,
        scratch_shapes=[pltpu.VMEM((tm + 2 * _HALO, D_MODEL), _BF16)],
        compiler_params=pltpu.CompilerParams(
            dimension_semantics=("arbitrary",), vmem_limit_bytes=_VMEM_LIMIT_BYTES),
        name="proj",
    )(x2d, x2d, x2d, gmix, wcb, wcc, wcx, wqt, wk, wvt, wg, convw, gq, gk, cos_t, sin_t, ck, sk)


def _attn_kernel(qt_ref, k_ref, vt_ref, o_ref, acc_scr, s_even, s_odd, mx_even, mx_odd):
    tq = qt_ref.shape[1]
    seq_len = k_ref.shape[0]
    tkv = _ATTN_TKV
    n_chunks = seq_len // tkv
    acc_scr[...] = jnp.zeros_like(acc_scr)

    def kv_slice(c):
        return pl.ds(pl.multiple_of(c * tkv, tkv), tkv)

    def scores(g, kc, s_ref, mx_ref):
        s = _dot(kc, qt_ref[g * HEAD_DIM:(g + 1) * HEAD_DIM, :])
        s_ref[g] = s
        mx_ref[g] = jnp.max(s, axis=0, keepdims=True)

    def softmax_pv(g, vc, s_ref, mx_ref, m, l):
        m_new = jnp.maximum(m, mx_ref[g])
        p = jnp.exp2(s_ref[g] - m_new)
        alpha = jnp.exp2(m - m_new)
        l = alpha * l + jnp.sum(p, axis=0, keepdims=True)
        acc_scr[g] = alpha * acc_scr[g] + _dot(vc, p.astype(_BF16))
        return m_new, l

    def stage(c_next, s_next, mx_next, c_cur, s_cur, mx_cur, state):
        kc = None if c_next is None else k_ref[kv_slice(c_next), :]
        vc = None if c_cur is None else vt_ref[:, kv_slice(c_cur)]
        new = []
        for g in range(GROUP):
            if c_next is not None:
                scores(g, kc, s_next, mx_next)
            if c_cur is not None:
                new.append(softmax_pv(g, vc, s_cur, mx_cur, *state[g]))
        return tuple(new) if new else state

    def body(j, state):
        c = 2 * j
        state = stage(c + 1, s_odd, mx_odd, c, s_even, mx_even, state)
        return stage(c + 2, s_even, mx_even, c + 1, s_odd, mx_odd, state)

    state = tuple((jnp.full((1, tq), -1e30, _F32), jnp.zeros((1, tq), _F32))
                  for _ in range(GROUP))
    state = stage(0, s_even, mx_even, None, None, None, state)
    state = lax.fori_loop(0, n_chunks // 2 - 1, body, state)
    state = stage(n_chunks - 1, s_odd, mx_odd, n_chunks - 2, s_even, mx_even, state)
    state = stage(None, None, None, n_chunks - 1, s_odd, mx_odd, state)
    for g in range(GROUP):
        o_ref[:, g * HEAD_DIM:(g + 1) * HEAD_DIM] = (acc_scr[g] / state[g][1]).T.astype(_BF16)


def _attn_call(qt, k, vt, batch, seq_len):
    n_tok = batch * seq_len
    tq = _ATTN_TQ
    assert seq_len % tq == 0 and seq_len % (2 * _ATTN_TKV) == 0
    q_tiles = seq_len // tq
    group_w = GROUP * HEAD_DIM
    return pl.pallas_call(
        _attn_kernel,
        grid=(batch, N_KV_HEADS, q_tiles),
        in_specs=[
            pl.BlockSpec((group_w, tq), lambda b, h, i: (h, b * q_tiles + i)),
            pl.BlockSpec((seq_len, HEAD_DIM), lambda b, h, i: (b, h)),
            pl.BlockSpec((HEAD_DIM, seq_len), lambda b, h, i: (h, b)),
        ],
        out_specs=pl.BlockSpec((tq, group_w), lambda b, h, i: (b * q_tiles + i, h)),
        out_shape=jax.ShapeDtypeStruct((n_tok, ATTN_WIDTH), _BF16),
        scratch_shapes=[pltpu.VMEM((GROUP, HEAD_DIM, tq), _F32),
                        pltpu.VMEM((GROUP, _ATTN_TKV, tq), _F32),
                        pltpu.VMEM((GROUP, _ATTN_TKV, tq), _F32),
                        pltpu.VMEM((GROUP, 1, tq), _F32),
                        pltpu.VMEM((GROUP, 1, tq), _F32)],
        compiler_params=pltpu.CompilerParams(
            dimension_semantics=("arbitrary", "arbitrary", "arbitrary"),
            vmem_limit_bytes=_VMEM_LIMIT_BYTES),
        name="attn",
    )(qt, k, vt)


def _post_kernel(x_ref, yc_ref, ya_ref, g_ref, wb0_ref, wb1_ref, wout_ref, gffn_ref, wfg_ref,
                 wfu_ref, wfo_ref, gfin_ref, o_ref, act_scr, *, final_norm):
    tm = x_ref.shape[0]
    groups = [slice(r0, r0 + tm // _POST_GROUPS) for r0 in range(0, tm, tm // _POST_GROUPS)]

    merged = []
    for rs in groups:
        g = g_ref[rs, :].astype(_F32)
        m = g[:, :D_MODEL] * _dot(yc_ref[rs, :], wb0_ref[...])
        m += g[:, D_MODEL:] * _dot(ya_ref[rs, :], wb1_ref[...])
        merged.append(m.astype(_BF16))
    xn = []
    for rs, m in zip(groups, merged):
        x1 = x_ref[rs, :] + _dot(m, wout_ref[...])
        o_ref[rs, :] = x1
        xn.append(_rms_rows(x1, gffn_ref[...]).astype(_BF16))
    for c0 in range(0, FF_HIDDEN, _FFN_CK):
        cs = slice(c0, c0 + _FFN_CK)
        for rs, xg in zip(groups, xn):
            gate = _dot(xg, wfg_ref[:, cs])
            up = _dot(xg, wfu_ref[:, cs])
            act_scr[rs, cs] = (gate * jax.nn.sigmoid(gate) * up).astype(_BF16)
    for rs in groups:
        x2 = o_ref[rs, :] + _dot(act_scr[rs, :], wfo_ref[...])
        if final_norm:
            x2 = _rms_rows(x2, gfin_ref[...])
        o_ref[rs, :] = x2


def _post_call(x2d, yconv, yattn, gates, wb0, wb1, wout, gffn, wfg, wfu, wfo, gfin, final_norm):
    n_tok = x2d.shape[0]
    tm = _POST_TM
    assert n_tok % tm == 0 and FF_HIDDEN % _FFN_CK == 0

    def row_spec(width):
        return pl.BlockSpec((tm, width), lambda i: (i, 0))

    return pl.pallas_call(
        functools.partial(_post_kernel, final_norm=final_norm),
        grid=(n_tok // tm,),
        in_specs=[
            row_spec(D_MODEL), row_spec(CONV_WIDTH), row_spec(ATTN_WIDTH),
            row_spec(N_BRANCH * D_MODEL),
            _resident(wb0.shape), _resident(wb1.shape), _resident(wout.shape),
            _resident(gffn.shape), _resident(wfg.shape), _resident(wfu.shape),
            _resident(wfo.shape), _resident(gfin.shape),
        ],
        out_specs=row_spec(D_MODEL),
        out_shape=jax.ShapeDtypeStruct((n_tok, D_MODEL), _F32),
        scratch_shapes=[pltpu.VMEM((tm, FF_HIDDEN), _BF16)],
        compiler_params=pltpu.CompilerParams(
            dimension_semantics=("arbitrary",), vmem_limit_bytes=_VMEM_LIMIT_BYTES),
        name="post",
    )(x2d, yconv, yattn, gates, wb0, wb1, wout, gffn, wfg, wfu, wfo, gfin)


_HEAD_PERM = np.concatenate([np.arange(0, HEAD_DIM, 2), np.arange(1, HEAD_DIM, 2)])


def _permute_heads(w, n_heads):
    return w.reshape(w.shape[0], n_heads, HEAD_DIM)[:, :, _HEAD_PERM].reshape(w.shape[0], -1)


def _layer_weights(w_in, conv_w, q_norm_g, k_norm_g, w_branch, w_out, w_ffn_in, w_ffn_out):
    o = np.cumsum([0, CONV_WIDTH, CONV_WIDTH, CONV_WIDTH, ATTN_WIDTH, KV_WIDTH, KV_WIDTH])
    wb = w_in.astype(_BF16)
    return dict(
        wcb=wb[:, o[0]:o[1]], wcc=wb[:, o[1]:o[2]], wcx=wb[:, o[2]:o[3]],
        wqt=_permute_heads(wb[:, o[3]:o[4]], N_HEADS).T,
        wk=_permute_heads(wb[:, o[4]:o[5]], N_KV_HEADS),
        wvt=wb[:, o[5]:o[6]].T,
        wg=wb[:, o[6]:],
        convw=conv_w,
        gq=q_norm_g[_HEAD_PERM].reshape(HEAD_DIM, 1),
        gk=k_norm_g[_HEAD_PERM].reshape(1, HEAD_DIM),
        wb0=w_branch[0].astype(_BF16), wb1=w_branch[1].astype(_BF16),
        wout=w_out.astype(_BF16),
        wfg=w_ffn_in[:, :FF_HIDDEN].astype(_BF16), wfu=w_ffn_in[:, FF_HIDDEN:].astype(_BF16),
        wfo=w_ffn_out.astype(_BF16),
    )


def _rope_tables(seq_len):
    n_rows = seq_len // GRID_W
    inv_freq = ROPE_THETA ** (-jnp.arange(0, ROPE_HALF, 2, dtype=_F32) / ROPE_HALF)
    row_ang = jnp.arange(n_rows, dtype=_F32)[:, None] * inv_freq[None, :]
    col_ang = jnp.arange(GRID_W, dtype=_F32)[:, None] * inv_freq[None, :]

    def expand(fn):
        return jnp.concatenate([jnp.repeat(fn(row_ang), GRID_W, axis=0),
                                jnp.tile(fn(col_ang), (n_rows, 1))], axis=-1)

    c, s = expand(jnp.cos), expand(jnp.sin)
    return dict(cos_t=c.T, sin_t=s.T,
                ck=jnp.concatenate([c, c], axis=-1), sk=jnp.concatenate([-s, s], axis=-1))


def _trunk(x, layers, norm_mix_g, norm_ffn_g, norm_final_g):
    batch, seq_len, _ = x.shape
    x2d = x.reshape(batch * seq_len, D_MODEL)
    rope = _rope_tables(seq_len)
    gfin = norm_final_g.reshape(1, D_MODEL)
    for l, w in enumerate(layers):
        yconv, qt, k, vt, gates = _proj_call(
            x2d, seq_len, norm_mix_g[l].reshape(1, D_MODEL), w["wcb"], w["wcc"], w["wcx"],
            w["wqt"], w["wk"], w["wvt"], w["wg"], w["convw"], w["gq"], w["gk"],
            rope["cos_t"], rope["sin_t"], rope["ck"], rope["sk"])
        yattn = _attn_call(qt, k, vt, batch, seq_len)
        x2d = _post_call(x2d, yconv, yattn, gates, w["wb0"], w["wb1"], w["wout"],
                         norm_ffn_g[l].reshape(1, D_MODEL), w["wfg"], w["wfu"], w["wfo"], gfin,
                         final_norm=(l == len(layers) - 1))
    return x2d.reshape(batch, seq_len, D_MODEL)


def kernel(x_prompt, x_sample, norm_mix_g, w_in, conv_w, q_norm_g, k_norm_g, w_branch, w_out,
           norm_ffn_g, w_ffn_in, w_ffn_out, norm_final_g):
    depth = w_in.shape[0]
    layers = [_layer_weights(w_in[l], conv_w[l], q_norm_g[l], k_norm_g[l], w_branch[l], w_out[l],
                             w_ffn_in[l], w_ffn_out[l]) for l in range(depth)]
    y_prompt = _trunk(x_prompt, layers, norm_mix_g, norm_ffn_g, norm_final_g)
    y_sample = _trunk(x_sample, layers, norm_mix_g, norm_ffn_g, norm_final_g)
    return (y_prompt, y_sample)
```

```python
import functools

import jax
import jax.numpy as jnp
from jax import lax
from jax.experimental import pallas as pl
from jax.experimental.pallas import tpu as pltpu

D_MODEL = 1024
GRID_W = 64
N_HEADS = 8
N_KV_HEADS = 2
HEAD_DIM = 128
GROUP = N_HEADS // N_KV_HEADS
ATTN_WIDTH = N_HEADS * HEAD_DIM
KV_WIDTH = N_KV_HEADS * HEAD_DIM
ROPE_HALF = HEAD_DIM // 2
ROPE_THETA = 10000.0
CONV_WIDTH = D_MODEL
N_BRANCH = 2
FF_HIDDEN = 2816
EPS = 1e-6

_IN_Q = 3 * CONV_WIDTH
_IN_K = _IN_Q + ATTN_WIDTH
_IN_V = _IN_K + KV_WIDTH
_IN_G = _IN_V + KV_WIDTH
_W_K = 3 * CONV_WIDTH
_W_G = _W_K + KV_WIDTH
_W_COLS = _W_G + N_BRANCH * D_MODEL

_V7X_VMEM_BYTES = 64 * 1024 * 1024
_VMEM_LIMIT_BYTES = _V7X_VMEM_BYTES - 8 * 1024 * 1024
_HALO = 16

_PROJ_TM = 512
_PROJ_CK = 512
_ATTN_TQ = 512
_ATTN_TKV = 512
_POST_TM = 512
_POST_GROUPS = 2
_FFN_CK = 256

_BF16 = jnp.bfloat16
_F32 = jnp.float32
_NT_DIMS = (((1,), (1,)), ((), ()))
_LOG2_E = 1.4426950408889634


def _dot(a, b):
    return jnp.dot(a, b, preferred_element_type=_F32)


def _dot_nt(a, b):
    return lax.dot_general(a, b, _NT_DIMS, preferred_element_type=_F32)


def _rms_rows(x, g):
    ms = jnp.mean(x * x, axis=-1, keepdims=True)
    return x * lax.rsqrt(ms + EPS) * g


def _resident(shape):
    nd = len(shape)
    return pl.BlockSpec(shape, lambda *_: (0,) * nd, pipeline_mode=pl.Buffered(1))


def _proj_kernel(xm_ref, xp_ref, xn_ref, gmix_ref, w_ref, wqvt_ref, convw_ref, gq_ref, gk_ref,
                 cst_ref, csk_ref, yconv_ref, qt_ref, k_ref, vt_ref, g_ref, xn_scr, *,
                 tiles_per_seq):
    tm = xm_ref.shape[0]
    i = pl.program_id(0)
    seq_first = (i % tiles_per_seq) == 0
    seq_last = (i % tiles_per_seq) == tiles_per_seq - 1
    gmix = gmix_ref[...]

    prev = _rms_rows(xp_ref[...], gmix)
    nxt = _rms_rows(xn_ref[...], gmix)
    xn_scr[0:_HALO, :] = jnp.where(seq_first, 0.0, prev).astype(_BF16)
    xn_scr[_HALO:_HALO + tm, :] = _rms_rows(xm_ref[...], gmix).astype(_BF16)
    xn_scr[_HALO + tm:, :] = jnp.where(seq_last, 0.0, nxt).astype(_BF16)

    xe = xn_scr[...]
    xm = xn_scr[_HALO:_HALO + tm, :]
    rows_ext = tm + 2 * _HALO

    for c0 in range(0, CONV_WIDTH, _PROJ_CK):
        cs = slice(c0, c0 + _PROJ_CK)
        u = (_dot(xe, w_ref[:, CONV_WIDTH + c0:CONV_WIDTH + c0 + _PROJ_CK])
             * _dot(xe, w_ref[:, 2 * CONV_WIDTH + c0:2 * CONV_WIDTH + c0 + _PROJ_CK]))
        u_prev = pltpu.roll(u, 1, axis=0)
        u_next = pltpu.roll(u, rows_ext - 1, axis=0)
        w = convw_ref[:, cs]
        conv = w[0:1, :] * u_prev + w[1:2, :] * u + w[2:3, :] * u_next
        cb = _dot(xm, w_ref[:, cs])
        yconv_ref[:, cs] = (cb * conv[_HALO:_HALO + tm, :]).astype(_BF16)

    cos_t = cst_ref[:ROPE_HALF, :]
    sin_t = cst_ref[ROPE_HALF:, :]
    gq = gq_ref[...]
    q_scale = HEAD_DIM ** -0.5 * _LOG2_E
    qt = _dot_nt(wqvt_ref[:ATTN_WIDTH, :], xm)
    for h in range(N_HEADS):
        r0 = h * HEAD_DIM
        qh = qt[r0:r0 + HEAD_DIM, :]
        ms = jnp.mean(qh * qh, axis=0, keepdims=True)
        qh = qh * (lax.rsqrt(ms + EPS) * q_scale) * gq
        x0 = qh[:ROPE_HALF, :]
        x1 = qh[ROPE_HALF:, :]
        qt_ref[r0:r0 + ROPE_HALF, :] = (x0 * cos_t - x1 * sin_t).astype(_BF16)
        qt_ref[r0 + ROPE_HALF:r0 + HEAD_DIM, :] = (x0 * sin_t + x1 * cos_t).astype(_BF16)

    kf = _dot(xm, w_ref[:, _W_K:_W_G])
    gk = gk_ref[...]
    ck = csk_ref[:, :HEAD_DIM]
    sk = csk_ref[:, HEAD_DIM:]
    for h in range(N_KV_HEADS):
        cs = slice(h * HEAD_DIM, (h + 1) * HEAD_DIM)
        kh = _rms_rows(kf[:, cs], gk)
        k_ref[:, cs] = (kh * ck + pltpu.roll(kh, ROPE_HALF, axis=1) * sk).astype(_BF16)

    vt_ref[...] = _dot_nt(wqvt_ref[ATTN_WIDTH:, :], xm).astype(_BF16)

    for c0 in range(0, N_BRANCH * D_MODEL, _PROJ_CK):
        g_ref[:, c0:c0 + _PROJ_CK] = jax.nn.sigmoid(
            _dot(xm, w_ref[:, _W_G + c0:_W_G + c0 + _PROJ_CK])).astype(_BF16)


def _proj_call(x2d, seq_len, gmix, w, wqvt, convw, gq, gk, cst, csk):
    n_tok = x2d.shape[0]
    tm = _PROJ_TM
    assert seq_len % tm == 0 and tm % _HALO == 0
    tiles_per_seq = seq_len // tm
    halo_per_tile = tm // _HALO
    n_halo_blocks = n_tok // _HALO

    def prev_map(i):
        return (jnp.maximum(i * halo_per_tile - 1, 0), 0)

    def next_map(i):
        return (jnp.minimum((i + 1) * halo_per_tile, n_halo_blocks - 1), 0)

    in_specs = [
        pl.BlockSpec((tm, D_MODEL), lambda i: (i, 0)),
        pl.BlockSpec((_HALO, D_MODEL), prev_map),
        pl.BlockSpec((_HALO, D_MODEL), next_map),
        _resident(gmix.shape), _resident(w.shape), _resident(wqvt.shape),
        _resident(convw.shape), _resident(gq.shape), _resident(gk.shape),
        pl.BlockSpec((HEAD_DIM, tm), lambda i: (0, i % tiles_per_seq)),
        pl.BlockSpec((tm, 2 * HEAD_DIM), lambda i: (i % tiles_per_seq, 0)),
    ]
    out_shape = (
        jax.ShapeDtypeStruct((n_tok, CONV_WIDTH), _BF16),
        jax.ShapeDtypeStruct((ATTN_WIDTH, n_tok), _BF16),
        jax.ShapeDtypeStruct((n_tok, KV_WIDTH), _BF16),
        jax.ShapeDtypeStruct((KV_WIDTH, n_tok), _BF16),
        jax.ShapeDtypeStruct((n_tok, N_BRANCH * D_MODEL), _BF16),
    )
    out_specs = (
        pl.BlockSpec((tm, CONV_WIDTH), lambda i: (i, 0)),
        pl.BlockSpec((ATTN_WIDTH, tm), lambda i: (0, i)),
        pl.BlockSpec((tm, KV_WIDTH), lambda i: (i, 0)),
        pl.BlockSpec((KV_WIDTH, tm), lambda i: (0, i)),
        pl.BlockSpec((tm, N_BRANCH * D_MODEL), lambda i: (i, 0)),
    )
    return pl.pallas_call(
        functools.partial(_proj_kernel, tiles_per_seq=tiles_per_seq),
        grid=(n_tok // tm,),
        in_specs=in_specs,
        out_specs=out_specs,
        out_shape=out_shape,
        scratch_shapes=[pltpu.VMEM((tm + 2 * _HALO, D_MODEL), _BF16)],
        compiler_params=pltpu.CompilerParams(
            dimension_semantics=("arbitrary",), vmem_limit_bytes=_VMEM_LIMIT_BYTES),
        name="proj",
    )(x2d, x2d, x2d, gmix, w, wqvt, convw, gq, gk, cst, csk)


def _attn_kernel(qt_ref, k_ref, vt_ref, o_ref, acc_scr, s_even, s_odd, mx_even, mx_odd):
    tq = qt_ref.shape[1]
    seq_len = k_ref.shape[0]
    tkv = _ATTN_TKV
    n_chunks = seq_len // tkv
    acc_scr[...] = jnp.zeros_like(acc_scr)

    def kv_slice(c):
        return pl.ds(pl.multiple_of(c * tkv, tkv), tkv)

    def scores(g, kc, s_ref, mx_ref):
        s = _dot(kc, qt_ref[g * HEAD_DIM:(g + 1) * HEAD_DIM, :])
        s_ref[g] = s
        mx_ref[g] = jnp.max(s, axis=0, keepdims=True)

    def softmax_pv(g, vc, s_ref, mx_ref, m, l):
        m_new = jnp.maximum(m, mx_ref[g])
        p = jnp.exp2(s_ref[g] - m_new)
        alpha = jnp.exp2(m - m_new)
        l = alpha * l + jnp.sum(p, axis=0, keepdims=True)
        acc_scr[g] = alpha * acc_scr[g] + _dot(vc, p.astype(_BF16))
        return m_new, l

    def stage(c_next, s_next, mx_next, c_cur, s_cur, mx_cur, state):
        kc = None if c_next is None else k_ref[kv_slice(c_next), :]
        vc = None if c_cur is None else vt_ref[:, kv_slice(c_cur)]
        new = []
        for g in range(GROUP):
            if c_next is not None:
                scores(g, kc, s_next, mx_next)
            if c_cur is not None:
                new.append(softmax_pv(g, vc, s_cur, mx_cur, *state[g]))
        return tuple(new) if new else state

    def body(j, state):
        c = 2 * j
        state = stage(c + 1, s_odd, mx_odd, c, s_even, mx_even, state)
        return stage(c + 2, s_even, mx_even, c + 1, s_odd, mx_odd, state)

    state = tuple((jnp.full((1, tq), -1e30, _F32), jnp.zeros((1, tq), _F32))
                  for _ in range(GROUP))
    state = stage(0, s_even, mx_even, None, None, None, state)
    state = lax.fori_loop(0, n_chunks // 2 - 1, body, state)
    state = stage(n_chunks - 1, s_odd, mx_odd, n_chunks - 2, s_even, mx_even, state)
    state = stage(None, None, None, n_chunks - 1, s_odd, mx_odd, state)
    for g in range(GROUP):
        o_ref[:, g * HEAD_DIM:(g + 1) * HEAD_DIM] = (acc_scr[g] / state[g][1]).T.astype(_BF16)


def _attn_call(qt, k, vt, batch, seq_len):
    n_tok = batch * seq_len
    tq = _ATTN_TQ
    assert seq_len % tq == 0 and seq_len % (2 * _ATTN_TKV) == 0
    q_tiles = seq_len // tq
    group_w = GROUP * HEAD_DIM
    return pl.pallas_call(
        _attn_kernel,
        grid=(batch, N_KV_HEADS, q_tiles),
        in_specs=[
            pl.BlockSpec((group_w, tq), lambda b, h, i: (h, b * q_tiles + i)),
            pl.BlockSpec((seq_len, HEAD_DIM), lambda b, h, i: (b, h)),
            pl.BlockSpec((HEAD_DIM, seq_len), lambda b, h, i: (h, b)),
        ],
        out_specs=pl.BlockSpec((tq, group_w), lambda b, h, i: (b * q_tiles + i, h)),
        out_shape=jax.ShapeDtypeStruct((n_tok, ATTN_WIDTH), _BF16),
        scratch_shapes=[pltpu.VMEM((GROUP, HEAD_DIM, tq), _F32),
                        pltpu.VMEM((GROUP, _ATTN_TKV, tq), _F32),
                        pltpu.VMEM((GROUP, _ATTN_TKV, tq), _F32),
                        pltpu.VMEM((GROUP, 1, tq), _F32),
                        pltpu.VMEM((GROUP, 1, tq), _F32)],
        compiler_params=pltpu.CompilerParams(
            dimension_semantics=("arbitrary", "arbitrary", "arbitrary"),
            vmem_limit_bytes=_VMEM_LIMIT_BYTES),
        name="attn",
    )(qt, k, vt)


def _post_kernel(x_ref, yc_ref, ya_ref, g_ref, wb_ref, wout_ref, gffn_ref, wfi_ref, wfo_ref,
                 gfin_ref, o_ref, act_scr, *, final_norm):
    tm = x_ref.shape[0]
    groups = [slice(r0, r0 + tm // _POST_GROUPS) for r0 in range(0, tm, tm // _POST_GROUPS)]

    merged = []
    for rs in groups:
        g = g_ref[rs, :].astype(_F32)
        m = g[:, :D_MODEL] * _dot(yc_ref[rs, :], wb_ref[0])
        m += g[:, D_MODEL:] * _dot(ya_ref[rs, :], wb_ref[1])
        merged.append(m.astype(_BF16))
    xn = []
    for rs, m in zip(groups, merged):
        x1 = x_ref[rs, :] + _dot(m, wout_ref[...])
        o_ref[rs, :] = x1
        xn.append(_rms_rows(x1, gffn_ref[...]).astype(_BF16))
    for c0 in range(0, FF_HIDDEN, _FFN_CK):
        for rs, xg in zip(groups, xn):
            gate = _dot(xg, wfi_ref[:, c0:c0 + _FFN_CK])
            up = _dot(xg, wfi_ref[:, FF_HIDDEN + c0:FF_HIDDEN + c0 + _FFN_CK])
            act_scr[rs, c0:c0 + _FFN_CK] = (gate * jax.nn.sigmoid(gate) * up).astype(_BF16)
    for rs in groups:
        x2 = o_ref[rs, :] + _dot(act_scr[rs, :], wfo_ref[...])
        if final_norm:
            x2 = _rms_rows(x2, gfin_ref[...])
        o_ref[rs, :] = x2


def _post_call(x2d, yconv, yattn, gates, wb, wout, gffn, wfi, wfo, gfin, final_norm):
    n_tok = x2d.shape[0]
    tm = _POST_TM
    assert n_tok % tm == 0 and FF_HIDDEN % _FFN_CK == 0 and tm % _POST_GROUPS == 0

    def row_spec(width):
        return pl.BlockSpec((tm, width), lambda i: (i, 0))

    return pl.pallas_call(
        functools.partial(_post_kernel, final_norm=final_norm),
        grid=(n_tok // tm,),
        in_specs=[
            row_spec(D_MODEL), row_spec(CONV_WIDTH), row_spec(ATTN_WIDTH),
            row_spec(N_BRANCH * D_MODEL),
            _resident(wb.shape), _resident(wout.shape), _resident(gffn.shape),
            _resident(wfi.shape), _resident(wfo.shape), _resident(gfin.shape),
        ],
        out_specs=row_spec(D_MODEL),
        out_shape=jax.ShapeDtypeStruct((n_tok, D_MODEL), _F32),
        scratch_shapes=[pltpu.VMEM((tm, FF_HIDDEN), _BF16)],
        compiler_params=pltpu.CompilerParams(
            dimension_semantics=("arbitrary",), vmem_limit_bytes=_VMEM_LIMIT_BYTES),
        name="post",
    )(x2d, yconv, yattn, gates, wb, wout, gffn, wfi, wfo, gfin)


def _split_pairs(w, n_heads):
    lead = w.shape[:-1]
    w = w.reshape(*lead, n_heads, ROPE_HALF, 2)
    return jnp.swapaxes(w, -1, -2).reshape(*lead, n_heads * HEAD_DIM)


def _layer_weights(w_in, conv_w, q_norm_g, k_norm_g, w_branch, w_out, w_ffn_in, w_ffn_out):
    wq = _split_pairs(w_in[:, _IN_Q:_IN_K], N_HEADS)
    wk = _split_pairs(w_in[:, _IN_K:_IN_V], N_KV_HEADS)
    return dict(
        w=jnp.concatenate([w_in[:, :_IN_Q], wk, w_in[:, _IN_G:]], axis=1).astype(_BF16),
        wqvt=jnp.concatenate([wq, w_in[:, _IN_V:_IN_G]], axis=1).T.astype(_BF16),
        convw=conv_w,
        gq=_split_pairs(q_norm_g, 1).reshape(HEAD_DIM, 1),
        gk=_split_pairs(k_norm_g, 1).reshape(1, HEAD_DIM),
        wb=w_branch.astype(_BF16), wout=w_out.astype(_BF16),
        wfi=w_ffn_in.astype(_BF16), wfo=w_ffn_out.astype(_BF16),
    )


def _rope_tables(seq_len):
    n_rows = seq_len // GRID_W
    inv_freq = ROPE_THETA ** (-jnp.arange(0, ROPE_HALF, 2, dtype=_F32) / ROPE_HALF)
    row_ang = jnp.arange(n_rows, dtype=_F32)[:, None] * inv_freq[None, :]
    col_ang = jnp.arange(GRID_W, dtype=_F32)[:, None] * inv_freq[None, :]

    def expand(fn):
        return jnp.concatenate([jnp.repeat(fn(row_ang), GRID_W, axis=0),
                                jnp.tile(fn(col_ang), (n_rows, 1))], axis=-1)

    c, s = expand(jnp.cos), expand(jnp.sin)
    return dict(cst=jnp.concatenate([c, s], axis=-1).T,
                csk=jnp.concatenate([c, c, -s, s], axis=-1))


def _trunk(x, layers, norm_mix_g, norm_ffn_g, norm_final_g):
    batch, seq_len, _ = x.shape
    x2d = x.reshape(batch * seq_len, D_MODEL)
    rope = _rope_tables(seq_len)
    gfin = norm_final_g.reshape(1, D_MODEL)
    for l, w in enumerate(layers):
        yconv, qt, k, vt, gates = _proj_call(
            x2d, seq_len, norm_mix_g[l].reshape(1, D_MODEL), w["w"], w["wqvt"], w["convw"],
            w["gq"], w["gk"], rope["cst"], rope["csk"])
        yattn = _attn_call(qt, k, vt, batch, seq_len)
        x2d = _post_call(x2d, yconv, yattn, gates, w["wb"], w["wout"],
                         norm_ffn_g[l].reshape(1, D_MODEL), w["wfi"], w["wfo"], gfin,
                         final_norm=(l == len(layers) - 1))
    return x2d.reshape(batch, seq_len, D_MODEL)


def kernel(x_prompt, x_sample, norm_mix_g, w_in, conv_w, q_norm_g, k_norm_g, w_branch, w_out,
           norm_ffn_g, w_ffn_in, w_ffn_out, norm_final_g):
    depth = w_in.shape[0]
    layers = [_layer_weights(w_in[l], conv_w[l], q_norm_g[l], k_norm_g[l], w_branch[l], w_out[l],
                             w_ffn_in[l], w_ffn_out[l]) for l in range(depth)]
    y_prompt = _trunk(x_prompt, layers, norm_mix_g, norm_ffn_g, norm_final_g)
    y_sample = _trunk(x_sample, layers, norm_mix_g, norm_ffn_g, norm_final_g)
    return (y_prompt, y_sample)
```

```python
import functools

import jax
import jax.numpy as jnp
from jax import lax
from jax.experimental import pallas as pl
from jax.experimental.pallas import tpu as pltpu

D_MODEL = 1024
GRID_W = 64
N_HEADS = 8
N_KV_HEADS = 2
HEAD_DIM = 128
GROUP = N_HEADS // N_KV_HEADS
ATTN_WIDTH = N_HEADS * HEAD_DIM
KV_WIDTH = N_KV_HEADS * HEAD_DIM
ROPE_HALF = HEAD_DIM // 2
ROPE_THETA = 10000.0
CONV_WIDTH = D_MODEL
N_BRANCH = 2
FF_HIDDEN = 2816
EPS = 1e-6

_IN_Q = 3 * CONV_WIDTH
_IN_K = _IN_Q + ATTN_WIDTH
_IN_V = _IN_K + KV_WIDTH
_IN_G = _IN_V + KV_WIDTH

_V7X_VMEM_BYTES = 64 * 1024 * 1024
_VMEM_LIMIT_BYTES = _V7X_VMEM_BYTES - 8 * 1024 * 1024
_HALO = 16

_PROJ_TM = 512
_PROJ_CK = 512
_ATTN_TQ = 512
_ATTN_TKV = 512
_POST_TM = 512
_POST_GROUPS = 2
_FFN_CK = 256

_BF16 = jnp.bfloat16
_F32 = jnp.float32
_NT_DIMS = (((1,), (1,)), ((), ()))
_LOG2_E = 1.4426950408889634


def _dot(a, b):
    return jnp.dot(a, b, preferred_element_type=_F32)


def _dot_nt(a, b):
    return lax.dot_general(a, b, _NT_DIMS, preferred_element_type=_F32)


def _rms_rows(x, g):
    ms = jnp.mean(x * x, axis=-1, keepdims=True)
    return x * lax.rsqrt(ms + EPS) * g


def _resident(shape):
    nd = len(shape)
    return pl.BlockSpec(shape, lambda *_: (0,) * nd, pipeline_mode=pl.Buffered(1))


def _proj_kernel(xm_ref, xp_ref, xn_ref, gmix_ref, w_ref, wqt_ref, wk_ref, wvt_ref, convw_ref,
                 gq_ref, gk_ref, cst_ref, csk_ref, yconv_ref, qt_ref, k_ref, vt_ref, g_ref, xn_scr,
                 *, tiles_per_seq):
    tm = xm_ref.shape[0]
    i = pl.program_id(0)
    seq_first = (i % tiles_per_seq) == 0
    seq_last = (i % tiles_per_seq) == tiles_per_seq - 1
    gmix = gmix_ref[...]

    prev = _rms_rows(xp_ref[...], gmix)
    nxt = _rms_rows(xn_ref[...], gmix)
    xn_scr[0:_HALO, :] = jnp.where(seq_first, 0.0, prev).astype(_BF16)
    xn_scr[_HALO:_HALO + tm, :] = _rms_rows(xm_ref[...], gmix).astype(_BF16)
    xn_scr[_HALO + tm:, :] = jnp.where(seq_last, 0.0, nxt).astype(_BF16)

    xe = xn_scr[...]
    xm = xn_scr[_HALO:_HALO + tm, :]
    rows_ext = tm + 2 * _HALO

    for c0 in range(0, CONV_WIDTH, _PROJ_CK):
        cs = slice(c0, c0 + _PROJ_CK)
        u = (_dot(xe, w_ref[:, CONV_WIDTH + c0:CONV_WIDTH + c0 + _PROJ_CK])
             * _dot(xe, w_ref[:, 2 * CONV_WIDTH + c0:2 * CONV_WIDTH + c0 + _PROJ_CK]))
        u_prev = pltpu.roll(u, 1, axis=0)
        u_next = pltpu.roll(u, rows_ext - 1, axis=0)
        w = convw_ref[:, cs]
        conv = w[0:1, :] * u_prev + w[1:2, :] * u + w[2:3, :] * u_next
        cb = _dot(xm, w_ref[:, cs])
        yconv_ref[:, cs] = (cb * conv[_HALO:_HALO + tm, :]).astype(_BF16)

    cos_t = cst_ref[:ROPE_HALF, :]
    sin_t = cst_ref[ROPE_HALF:, :]
    gq = gq_ref[...]
    q_scale = HEAD_DIM ** -0.5 * _LOG2_E
    qt = _dot_nt(wqt_ref[...], xm)
    for h in range(N_HEADS):
        r0 = h * HEAD_DIM
        qh = qt[r0:r0 + HEAD_DIM, :]
        ms = jnp.mean(qh * qh, axis=0, keepdims=True)
        qh = qh * (lax.rsqrt(ms + EPS) * q_scale) * gq
        x0 = qh[:ROPE_HALF, :]
        x1 = qh[ROPE_HALF:, :]
        qt_ref[r0:r0 + ROPE_HALF, :] = (x0 * cos_t - x1 * sin_t).astype(_BF16)
        qt_ref[r0 + ROPE_HALF:r0 + HEAD_DIM, :] = (x0 * sin_t + x1 * cos_t).astype(_BF16)

    kf = _dot(xm, wk_ref[...])
    gk = gk_ref[...]
    ck = csk_ref[:, :HEAD_DIM]
    sk = csk_ref[:, HEAD_DIM:]
    for h in range(N_KV_HEADS):
        cs = slice(h * HEAD_DIM, (h + 1) * HEAD_DIM)
        kh = _rms_rows(kf[:, cs], gk)
        k_ref[:, cs] = (kh * ck + pltpu.roll(kh, ROPE_HALF, axis=1) * sk).astype(_BF16)

    vt_ref[...] = _dot_nt(wvt_ref[...], xm).astype(_BF16)

    for c0 in range(0, N_BRANCH * D_MODEL, _PROJ_CK):
        g_ref[:, c0:c0 + _PROJ_CK] = jax.nn.sigmoid(
            _dot(xm, w_ref[:, _IN_G + c0:_IN_G + c0 + _PROJ_CK])).astype(_BF16)


def _proj_call(x2d, seq_len, gmix, w, wqt, wk, wvt, convw, gq, gk, cst, csk):
    n_tok = x2d.shape[0]
    tm = _PROJ_TM
    assert seq_len % tm == 0 and tm % _HALO == 0
    tiles_per_seq = seq_len // tm
    halo_per_tile = tm // _HALO
    n_halo_blocks = n_tok // _HALO

    def prev_map(i):
        return (jnp.maximum(i * halo_per_tile - 1, 0), 0)

    def next_map(i):
        return (jnp.minimum((i + 1) * halo_per_tile, n_halo_blocks - 1), 0)

    in_specs = [
        pl.BlockSpec((tm, D_MODEL), lambda i: (i, 0)),
        pl.BlockSpec((_HALO, D_MODEL), prev_map),
        pl.BlockSpec((_HALO, D_MODEL), next_map),
        _resident(gmix.shape), _resident(w.shape), _resident(wqt.shape), _resident(wk.shape),
        _resident(wvt.shape),
        _resident(convw.shape), _resident(gq.shape), _resident(gk.shape),
        pl.BlockSpec((HEAD_DIM, tm), lambda i: (0, i % tiles_per_seq)),
        pl.BlockSpec((tm, 2 * HEAD_DIM), lambda i: (i % tiles_per_seq, 0)),
    ]
    out_shape = (
        jax.ShapeDtypeStruct((n_tok, CONV_WIDTH), _BF16),
        jax.ShapeDtypeStruct((ATTN_WIDTH, n_tok), _BF16),
        jax.ShapeDtypeStruct((n_tok, KV_WIDTH), _BF16),
        jax.ShapeDtypeStruct((KV_WIDTH, n_tok), _BF16),
        jax.ShapeDtypeStruct((n_tok, N_BRANCH * D_MODEL), _BF16),
    )
    out_specs = (
        pl.BlockSpec((tm, CONV_WIDTH), lambda i: (i, 0)),
        pl.BlockSpec((ATTN_WIDTH, tm), lambda i: (0, i)),
        pl.BlockSpec((tm, KV_WIDTH), lambda i: (i, 0)),
        pl.BlockSpec((KV_WIDTH, tm), lambda i: (0, i)),
        pl.BlockSpec((tm, N_BRANCH * D_MODEL), lambda i: (i, 0)),
    )
    return pl.pallas_call(
        functools.partial(_proj_kernel, tiles_per_seq=tiles_per_seq),
        grid=(n_tok // tm,),
        in_specs=in_specs,
        out_specs=out_specs,
        out_shape=out_shape,
        scratch_shapes=[pltpu.VMEM((tm + 2 * _HALO, D_MODEL), _BF16)],
        compiler_params=pltpu.CompilerParams(
            dimension_semantics=("arbitrary",), vmem_limit_bytes=_VMEM_LIMIT_BYTES),
        name="proj",
    )(x2d, x2d, x2d, gmix, w, wqt, wk, wvt, convw, gq, gk, cst, csk)


def _attn_kernel(qt_ref, k_ref, vt_ref, o_ref, acc_scr, s_even, s_odd, mx_even, mx_odd):
    tq = qt_ref.shape[1]
    seq_len = k_ref.shape[0]
    tkv = _ATTN_TKV
    n_chunks = seq_len // tkv
    acc_scr[...] = jnp.zeros_like(acc_scr)

    def kv_slice(c):
        return pl.ds(pl.multiple_of(c * tkv, tkv), tkv)

    def scores(g, kc, s_ref, mx_ref):
        s = _dot(kc, qt_ref[g * HEAD_DIM:(g + 1) * HEAD_DIM, :])
        s_ref[g] = s
        mx_ref[g] = jnp.max(s, axis=0, keepdims=True)

    def softmax_pv(g, vc, s_ref, mx_ref, m, l):
        m_new = jnp.maximum(m, mx_ref[g])
        p = jnp.exp2(s_ref[g] - m_new)
        alpha = jnp.exp2(m - m_new)
        l = alpha * l + jnp.sum(p, axis=0, keepdims=True)
        acc_scr[g] = alpha * acc_scr[g] + _dot(vc, p.astype(_BF16))
        return m_new, l

    def stage(c_next, s_next, mx_next, c_cur, s_cur, mx_cur, state):
        kc = None if c_next is None else k_ref[kv_slice(c_next), :]
        vc = None if c_cur is None else vt_ref[:, kv_slice(c_cur)]
        new = []
        for g in range(GROUP):
            if c_next is not None:
                scores(g, kc, s_next, mx_next)
            if c_cur is not None:
                new.append(softmax_pv(g, vc, s_cur, mx_cur, *state[g]))
        return tuple(new) if new else state

    def body(j, state):
        c = 2 * j
        state = stage(c + 1, s_odd, mx_odd, c, s_even, mx_even, state)
        return stage(c + 2, s_even, mx_even, c + 1, s_odd, mx_odd, state)

    state = tuple((jnp.full((1, tq), -1e30, _F32), jnp.zeros((1, tq), _F32))
                  for _ in range(GROUP))
    state = stage(0, s_even, mx_even, None, None, None, state)
    state = lax.fori_loop(0, n_chunks // 2 - 1, body, state)
    state = stage(n_chunks - 1, s_odd, mx_odd, n_chunks - 2, s_even, mx_even, state)
    state = stage(None, None, None, n_chunks - 1, s_odd, mx_odd, state)
    for g in range(GROUP):
        o_ref[:, g * HEAD_DIM:(g + 1) * HEAD_DIM] = (acc_scr[g] / state[g][1]).T.astype(_BF16)


def _attn_call(qt, k, vt, batch, seq_len):
    n_tok = batch * seq_len
    tq = _ATTN_TQ
    assert seq_len % tq == 0 and seq_len % (2 * _ATTN_TKV) == 0
    q_tiles = seq_len // tq
    group_w = GROUP * HEAD_DIM
    return pl.pallas_call(
        _attn_kernel,
        grid=(batch, N_KV_HEADS, q_tiles),
        in_specs=[
            pl.BlockSpec((group_w, tq), lambda b, h, i: (h, b * q_tiles + i)),
            pl.BlockSpec((seq_len, HEAD_DIM), lambda b, h, i: (b, h)),
            pl.BlockSpec((HEAD_DIM, seq_len), lambda b, h, i: (h, b)),
        ],
        out_specs=pl.BlockSpec((tq, group_w), lambda b, h, i: (b * q_tiles + i, h)),
        out_shape=jax.ShapeDtypeStruct((n_tok, ATTN_WIDTH), _BF16),
        scratch_shapes=[pltpu.VMEM((GROUP, HEAD_DIM, tq), _F32),
                        pltpu.VMEM((GROUP, _ATTN_TKV, tq), _F32),
                        pltpu.VMEM((GROUP, _ATTN_TKV, tq), _F32),
                        pltpu.VMEM((GROUP, 1, tq), _F32),
                        pltpu.VMEM((GROUP, 1, tq), _F32)],
        compiler_params=pltpu.CompilerParams(
            dimension_semantics=("arbitrary", "arbitrary", "arbitrary"),
            vmem_limit_bytes=_VMEM_LIMIT_BYTES),
        name="attn",
    )(qt, k, vt)


def _post_kernel(x_ref, yc_ref, ya_ref, g_ref, wb_ref, wout_ref, gffn_ref, wfi_ref, wfo_ref,
                 gfin_ref, o_ref, act_scr, *, final_norm):
    tm = x_ref.shape[0]
    groups = [slice(r0, r0 + tm // _POST_GROUPS) for r0 in range(0, tm, tm // _POST_GROUPS)]

    merged = []
    for rs in groups:
        g = g_ref[rs, :].astype(_F32)
        m = g[:, :D_MODEL] * _dot(yc_ref[rs, :], wb_ref[0])
        m += g[:, D_MODEL:] * _dot(ya_ref[rs, :], wb_ref[1])
        merged.append(m.astype(_BF16))
    xn = []
    for rs, m in zip(groups, merged):
        x1 = x_ref[rs, :] + _dot(m, wout_ref[...])
        o_ref[rs, :] = x1
        xn.append(_rms_rows(x1, gffn_ref[...]).astype(_BF16))
    for c0 in range(0, FF_HIDDEN, _FFN_CK):
        for rs, xg in zip(groups, xn):
            gate = _dot(xg, wfi_ref[:, c0:c0 + _FFN_CK])
            up = _dot(xg, wfi_ref[:, FF_HIDDEN + c0:FF_HIDDEN + c0 + _FFN_CK])
            act_scr[rs, c0:c0 + _FFN_CK] = (gate * jax.nn.sigmoid(gate) * up).astype(_BF16)
    for rs in groups:
        x2 = o_ref[rs, :] + _dot(act_scr[rs, :], wfo_ref[...])
        if final_norm:
            x2 = _rms_rows(x2, gfin_ref[...])
        o_ref[rs, :] = x2


def _post_call(x2d, yconv, yattn, gates, wb, wout, gffn, wfi, wfo, gfin, final_norm):
    n_tok = x2d.shape[0]
    tm = _POST_TM
    assert n_tok % tm == 0 and FF_HIDDEN % _FFN_CK == 0 and tm % _POST_GROUPS == 0

    def row_spec(width):
        return pl.BlockSpec((tm, width), lambda i: (i, 0))

    return pl.pallas_call(
        functools.partial(_post_kernel, final_norm=final_norm),
        grid=(n_tok // tm,),
        in_specs=[
            row_spec(D_MODEL), row_spec(CONV_WIDTH), row_spec(ATTN_WIDTH),
            row_spec(N_BRANCH * D_MODEL),
            _resident(wb.shape), _resident(wout.shape), _resident(gffn.shape),
            _resident(wfi.shape), _resident(wfo.shape), _resident(gfin.shape),
        ],
        out_specs=row_spec(D_MODEL),
        out_shape=jax.ShapeDtypeStruct((n_tok, D_MODEL), _F32),
        scratch_shapes=[pltpu.VMEM((tm, FF_HIDDEN), _BF16)],
        compiler_params=pltpu.CompilerParams(
            dimension_semantics=("arbitrary",), vmem_limit_bytes=_VMEM_LIMIT_BYTES),
        name="post",
    )(x2d, yconv, yattn, gates, wb, wout, gffn, wfi, wfo, gfin)


def _split_pairs(w, n_heads):
    lead = w.shape[:-1]
    w = w.reshape(*lead, n_heads, ROPE_HALF, 2)
    return jnp.swapaxes(w, -1, -2).reshape(*lead, n_heads * HEAD_DIM)


def _layer_weights(w_in, conv_w, q_norm_g, k_norm_g, w_branch, w_out, w_ffn_in, w_ffn_out):
    return dict(
        w=w_in.astype(_BF16),
        wqt=_split_pairs(w_in[:, _IN_Q:_IN_K], N_HEADS).T.astype(_BF16),
        wk=_split_pairs(w_in[:, _IN_K:_IN_V], N_KV_HEADS).astype(_BF16),
        wvt=w_in[:, _IN_V:_IN_G].T.astype(_BF16),
        convw=conv_w,
        gq=_split_pairs(q_norm_g, 1).reshape(HEAD_DIM, 1),
        gk=_split_pairs(k_norm_g, 1).reshape(1, HEAD_DIM),
        wb=w_branch.astype(_BF16), wout=w_out.astype(_BF16),
        wfi=w_ffn_in.astype(_BF16), wfo=w_ffn_out.astype(_BF16),
    )


def _rope_tables(seq_len):
    n_rows = seq_len // GRID_W
    inv_freq = ROPE_THETA ** (-jnp.arange(0, ROPE_HALF, 2, dtype=_F32) / ROPE_HALF)
    row_ang = jnp.arange(n_rows, dtype=_F32)[:, None] * inv_freq[None, :]
    col_ang = jnp.arange(GRID_W, dtype=_F32)[:, None] * inv_freq[None, :]

    def lanes(ang):
        c = jnp.tile(jnp.cos(ang), (1, 4))
        s = jnp.tile(jnp.sin(ang), (1, 2))
        return jnp.concatenate([c, -s, s], axis=-1)

    row_k, col_k = lanes(row_ang), lanes(col_ang)
    from_row = (jnp.arange(2 * HEAD_DIM) % ROPE_HALF) < ROPE_HALF // 2
    csk = jnp.where(from_row[None, None, :], row_k[:, None, :], col_k[None, :, :])
    row_t = jnp.concatenate([row_k[:, :ROPE_HALF], row_k[:, -ROPE_HALF:]], axis=-1).T
    col_t = jnp.concatenate([col_k[:, :ROPE_HALF], col_k[:, -ROPE_HALF:]], axis=-1).T
    cst = jnp.where(from_row[:HEAD_DIM, None, None], row_t[:, :, None], col_t[:, None, :])
    return dict(cst=cst.reshape(HEAD_DIM, seq_len), csk=csk.reshape(seq_len, 2 * HEAD_DIM))


def _trunk(x, layers, rope, norm_mix_g, norm_ffn_g, norm_final_g):
    batch, seq_len, _ = x.shape
    x2d = x.reshape(batch * seq_len, D_MODEL)
    gfin = norm_final_g.reshape(1, D_MODEL)
    for l, w in enumerate(layers):
        yconv, qt, k, vt, gates = _proj_call(
            x2d, seq_len, norm_mix_g[l].reshape(1, D_MODEL), w["w"], w["wqt"], w["wk"], w["wvt"],
            w["convw"], w["gq"], w["gk"], rope["cst"], rope["csk"])
        yattn = _attn_call(qt, k, vt, batch, seq_len)
        x2d = _post_call(x2d, yconv, yattn, gates, w["wb"], w["wout"],
                         norm_ffn_g[l].reshape(1, D_MODEL), w["wfi"], w["wfo"], gfin,
                         final_norm=(l == len(layers) - 1))
    return x2d.reshape(batch, seq_len, D_MODEL)


def kernel(x_prompt, x_sample, norm_mix_g, w_in, conv_w, q_norm_g, k_norm_g, w_branch, w_out,
           norm_ffn_g, w_ffn_in, w_ffn_out, norm_final_g):
    depth = w_in.shape[0]
    layers = [_layer_weights(w_in[l], conv_w[l], q_norm_g[l], k_norm_g[l], w_branch[l], w_out[l],
                             w_ffn_in[l], w_ffn_out[l]) for l in range(depth)]
    rope = _rope_tables(max(x_prompt.shape[1], x_sample.shape[1]))
    y_prompt = _trunk(x_prompt, layers, rope, norm_mix_g, norm_ffn_g, norm_final_g)
    y_sample = _trunk(x_sample, layers, rope, norm_mix_g, norm_ffn_g, norm_final_g)
    return (y_prompt, y_sample)
```

```python
import functools

import jax
import jax.numpy as jnp
from jax import lax
from jax.experimental import pallas as pl
from jax.experimental.pallas import tpu as pltpu

D_MODEL = 1024
GRID_W = 64
N_HEADS = 8
N_KV_HEADS = 2
HEAD_DIM = 128
GROUP = N_HEADS // N_KV_HEADS
ATTN_WIDTH = N_HEADS * HEAD_DIM
KV_WIDTH = N_KV_HEADS * HEAD_DIM
ROPE_HALF = HEAD_DIM // 2
ROPE_THETA = 10000.0
CONV_WIDTH = D_MODEL
N_BRANCH = 2
FF_HIDDEN = 2816
EPS = 1e-6

_IN_Q = 3 * CONV_WIDTH
_IN_K = _IN_Q + ATTN_WIDTH
_IN_V = _IN_K + KV_WIDTH
_IN_G = _IN_V + KV_WIDTH

_V7X_VMEM_BYTES = 64 * 1024 * 1024
_VMEM_LIMIT_BYTES = _V7X_VMEM_BYTES - 8 * 1024 * 1024
_HALO = 16

_PROJ_TM = 512
_PROJ_CK = 512
_ATTN_TQ = 512
_ATTN_TKV = 512
_POST_TM = 512
_POST_GROUPS = 2
_FFN_CK = 256

_BF16 = jnp.bfloat16
_F32 = jnp.float32
_NT_DIMS = (((1,), (1,)), ((), ()))
_LOG2_E = 1.4426950408889634


def _dot(a, b):
    return jnp.dot(a, b, preferred_element_type=_F32)


def _dot_nt(a, b):
    return lax.dot_general(a, b, _NT_DIMS, preferred_element_type=_F32)


def _rms_rows(x, g):
    ms = jnp.mean(x * x, axis=-1, keepdims=True)
    return x * lax.rsqrt(ms + EPS) * g


def _sigmoid(x):
    return 0.5 * jnp.tanh(0.5 * x) + 0.5


def _resident(shape):
    nd = len(shape)
    return pl.BlockSpec(shape, lambda *_: (0,) * nd, pipeline_mode=pl.Buffered(1))


def _proj_kernel(xm_ref, xp_ref, xn_ref, gmix_ref, w_ref, wqt_ref, wk_ref, wvt_ref, convw_ref,
                 gq_ref, gk_ref, cs_ref, yconv_ref, qt_ref, k_ref, vt_ref, g_ref, xn_scr,
                 *, tiles_per_seq):
    tm = xm_ref.shape[0]
    i = pl.program_id(0)
    seq_first = (i % tiles_per_seq) == 0
    seq_last = (i % tiles_per_seq) == tiles_per_seq - 1
    gmix = gmix_ref[...]

    prev = _rms_rows(xp_ref[...], gmix)
    nxt = _rms_rows(xn_ref[...], gmix)
    xn_scr[0:_HALO, :] = jnp.where(seq_first, 0.0, prev).astype(_BF16)
    xn_scr[_HALO:_HALO + tm, :] = _rms_rows(xm_ref[...], gmix).astype(_BF16)
    xn_scr[_HALO + tm:, :] = jnp.where(seq_last, 0.0, nxt).astype(_BF16)

    xe = xn_scr[...]
    xm = xn_scr[_HALO:_HALO + tm, :]
    rows_ext = tm + 2 * _HALO

    for c0 in range(0, CONV_WIDTH, _PROJ_CK):
        cs = slice(c0, c0 + _PROJ_CK)
        u = (_dot(xe, w_ref[:, CONV_WIDTH + c0:CONV_WIDTH + c0 + _PROJ_CK])
             * _dot(xe, w_ref[:, 2 * CONV_WIDTH + c0:2 * CONV_WIDTH + c0 + _PROJ_CK]))
        u_prev = pltpu.roll(u, 1, axis=0)
        u_next = pltpu.roll(u, rows_ext - 1, axis=0)
        w = convw_ref[:, cs]
        conv = w[0:1, :] * u_prev + w[1:2, :] * u + w[2:3, :] * u_next
        cb = _dot(xm, w_ref[:, cs])
        yconv_ref[:, cs] = (cb * conv[_HALO:_HALO + tm, :]).astype(_BF16)

    cs = cs_ref[...]
    cs_t = cs.T
    cos_t = cs_t[:ROPE_HALF, :]
    sin_t = cs_t[ROPE_HALF:, :]
    gq = gq_ref[...]
    q_scale = HEAD_DIM ** -0.5 * _LOG2_E
    qt = _dot_nt(wqt_ref[...], xm)
    for h in range(N_HEADS):
        r0 = h * HEAD_DIM
        qh = qt[r0:r0 + HEAD_DIM, :]
        ms = jnp.mean(qh * qh, axis=0, keepdims=True)
        qh = qh * (lax.rsqrt(ms + EPS) * q_scale) * gq
        x0 = qh[:ROPE_HALF, :]
        x1 = qh[ROPE_HALF:, :]
        qt_ref[r0:r0 + ROPE_HALF, :] = (x0 * cos_t - x1 * sin_t).astype(_BF16)
        qt_ref[r0 + ROPE_HALF:r0 + HEAD_DIM, :] = (x0 * sin_t + x1 * cos_t).astype(_BF16)

    kf = _dot(xm, wk_ref[...])
    gk = gk_ref[...]
    sc = pltpu.roll(cs, ROPE_HALF, axis=1)
    first_half = lax.broadcasted_iota(jnp.int32, cs.shape, 1) < ROPE_HALF
    ck = jnp.where(first_half, cs, sc)
    sk = jnp.where(first_half, -sc, cs)
    for h in range(N_KV_HEADS):
        cs = slice(h * HEAD_DIM, (h + 1) * HEAD_DIM)
        kh = _rms_rows(kf[:, cs], gk)
        k_ref[:, cs] = (kh * ck + pltpu.roll(kh, ROPE_HALF, axis=1) * sk).astype(_BF16)

    vt_ref[...] = _dot_nt(wvt_ref[...], xm).astype(_BF16)

    for c0 in range(0, N_BRANCH * D_MODEL, _PROJ_CK):
        g_ref[:, c0:c0 + _PROJ_CK] = _sigmoid(
            _dot(xm, w_ref[:, _IN_G + c0:_IN_G + c0 + _PROJ_CK])).astype(_BF16)


def _proj_call(x2d, seq_len, gmix, w, wqt, wk, wvt, convw, gq, gk, cs):
    n_tok = x2d.shape[0]
    tm = _PROJ_TM
    assert seq_len % tm == 0 and tm % _HALO == 0
    tiles_per_seq = seq_len // tm
    halo_per_tile = tm // _HALO
    n_halo_blocks = n_tok // _HALO

    def prev_map(i):
        return (jnp.maximum(i * halo_per_tile - 1, 0), 0)

    def next_map(i):
        return (jnp.minimum((i + 1) * halo_per_tile, n_halo_blocks - 1), 0)

    in_specs = [
        pl.BlockSpec((tm, D_MODEL), lambda i: (i, 0)),
        pl.BlockSpec((_HALO, D_MODEL), prev_map),
        pl.BlockSpec((_HALO, D_MODEL), next_map),
        _resident(gmix.shape), _resident(w.shape), _resident(wqt.shape), _resident(wk.shape),
        _resident(wvt.shape),
        _resident(convw.shape), _resident(gq.shape), _resident(gk.shape),
        pl.BlockSpec((tm, HEAD_DIM), lambda i: (i % tiles_per_seq, 0)),
    ]
    out_shape = (
        jax.ShapeDtypeStruct((n_tok, CONV_WIDTH), _BF16),
        jax.ShapeDtypeStruct((ATTN_WIDTH, n_tok), _BF16),
        jax.ShapeDtypeStruct((n_tok, KV_WIDTH), _BF16),
        jax.ShapeDtypeStruct((KV_WIDTH, n_tok), _BF16),
        jax.ShapeDtypeStruct((n_tok, N_BRANCH * D_MODEL), _BF16),
    )
    out_specs = (
        pl.BlockSpec((tm, CONV_WIDTH), lambda i: (i, 0)),
        pl.BlockSpec((ATTN_WIDTH, tm), lambda i: (0, i)),
        pl.BlockSpec((tm, KV_WIDTH), lambda i: (i, 0)),
        pl.BlockSpec((KV_WIDTH, tm), lambda i: (0, i)),
        pl.BlockSpec((tm, N_BRANCH * D_MODEL), lambda i: (i, 0)),
    )
    return pl.pallas_call(
        functools.partial(_proj_kernel, tiles_per_seq=tiles_per_seq),
        grid=(n_tok // tm,),
        in_specs=in_specs,
        out_specs=out_specs,
        out_shape=out_shape,
        scratch_shapes=[pltpu.VMEM((tm + 2 * _HALO, D_MODEL), _BF16)],
        compiler_params=pltpu.CompilerParams(
            dimension_semantics=("arbitrary",), vmem_limit_bytes=_VMEM_LIMIT_BYTES),
        name="proj",
    )(x2d, x2d, x2d, gmix, w, wqt, wk, wvt, convw, gq, gk, cs)


def _attn_kernel(qt_ref, k_ref, vt_ref, o_ref, acc_scr, s_even, s_odd, mx_even, mx_odd):
    tq = qt_ref.shape[1]
    seq_len = k_ref.shape[0]
    tkv = _ATTN_TKV
    n_chunks = seq_len // tkv
    acc_scr[...] = jnp.zeros_like(acc_scr)

    def kv_slice(c):
        return pl.ds(pl.multiple_of(c * tkv, tkv), tkv)

    def scores(g, kc, s_ref, mx_ref):
        s = _dot(kc, qt_ref[g * HEAD_DIM:(g + 1) * HEAD_DIM, :])
        s_ref[g] = s
        mx_ref[g] = jnp.max(s, axis=0, keepdims=True)

    def softmax_pv(g, vc, s_ref, mx_ref, m, l):
        m_new = jnp.maximum(m, mx_ref[g])
        p = jnp.exp2(s_ref[g] - m_new)
        alpha = jnp.exp2(m - m_new)
        l = alpha * l + jnp.sum(p, axis=0, keepdims=True)
        acc_scr[g] = alpha * acc_scr[g] + _dot(vc, p.astype(_BF16))
        return m_new, l

    def stage(c_next, s_next, mx_next, c_cur, s_cur, mx_cur, state):
        kc = None if c_next is None else k_ref[kv_slice(c_next), :]
        vc = None if c_cur is None else vt_ref[:, kv_slice(c_cur)]
        new = []
        for g in range(GROUP):
            if c_next is not None:
                scores(g, kc, s_next, mx_next)
            if c_cur is not None:
                new.append(softmax_pv(g, vc, s_cur, mx_cur, *state[g]))
        return tuple(new) if new else state

    def body(j, state):
        c = 2 * j
        state = stage(c + 1, s_odd, mx_odd, c, s_even, mx_even, state)
        return stage(c + 2, s_even, mx_even, c + 1, s_odd, mx_odd, state)

    state = tuple((jnp.full((1, tq), -1e30, _F32), jnp.zeros((1, tq), _F32))
                  for _ in range(GROUP))
    state = stage(0, s_even, mx_even, None, None, None, state)
    state = lax.fori_loop(0, n_chunks // 2 - 1, body, state)
    state = stage(n_chunks - 1, s_odd, mx_odd, n_chunks - 2, s_even, mx_even, state)
    state = stage(None, None, None, n_chunks - 1, s_odd, mx_odd, state)
    for g in range(GROUP):
        o_ref[:, g * HEAD_DIM:(g + 1) * HEAD_DIM] = (acc_scr[g] / state[g][1]).T.astype(_BF16)


def _attn_call(qt, k, vt, batch, seq_len):
    n_tok = batch * seq_len
    tq = _ATTN_TQ
    assert seq_len % tq == 0 and seq_len % (2 * _ATTN_TKV) == 0
    q_tiles = seq_len // tq
    group_w = GROUP * HEAD_DIM
    return pl.pallas_call(
        _attn_kernel,
        grid=(batch, N_KV_HEADS, q_tiles),
        in_specs=[
            pl.BlockSpec((group_w, tq), lambda b, h, i: (h, b * q_tiles + i)),
            pl.BlockSpec((seq_len, HEAD_DIM), lambda b, h, i: (b, h)),
            pl.BlockSpec((HEAD_DIM, seq_len), lambda b, h, i: (h, b)),
        ],
        out_specs=pl.BlockSpec((tq, group_w), lambda b, h, i: (b * q_tiles + i, h)),
        out_shape=jax.ShapeDtypeStruct((n_tok, ATTN_WIDTH), _BF16),
        scratch_shapes=[pltpu.VMEM((GROUP, HEAD_DIM, tq), _F32),
                        pltpu.VMEM((GROUP, _ATTN_TKV, tq), _F32),
                        pltpu.VMEM((GROUP, _ATTN_TKV, tq), _F32),
                        pltpu.VMEM((GROUP, 1, tq), _F32),
                        pltpu.VMEM((GROUP, 1, tq), _F32)],
        compiler_params=pltpu.CompilerParams(
            dimension_semantics=("arbitrary", "arbitrary", "arbitrary"),
            vmem_limit_bytes=_VMEM_LIMIT_BYTES),
        name="attn",
    )(qt, k, vt)


def _post_kernel(x_ref, yc_ref, ya_ref, g_ref, wb_ref, wout_ref, gffn_ref, wfi_ref, wfo_ref,
                 gfin_ref, o_ref, act_scr, *, final_norm):
    tm = x_ref.shape[0]
    groups = [slice(r0, r0 + tm // _POST_GROUPS) for r0 in range(0, tm, tm // _POST_GROUPS)]

    merged = []
    for rs in groups:
        g = g_ref[rs, :].astype(_F32)
        m = g[:, :D_MODEL] * _dot(yc_ref[rs, :], wb_ref[0])
        m += g[:, D_MODEL:] * _dot(ya_ref[rs, :], wb_ref[1])
        merged.append(m.astype(_BF16))
    xn = []
    for rs, m in zip(groups, merged):
        x1 = x_ref[rs, :] + _dot(m, wout_ref[...])
        o_ref[rs, :] = x1
        xn.append(_rms_rows(x1, gffn_ref[...]).astype(_BF16))
    for c0 in range(0, FF_HIDDEN, _FFN_CK):
        for rs, xg in zip(groups, xn):
            gate = _dot(xg, wfi_ref[:, c0:c0 + _FFN_CK])
            up = _dot(xg, wfi_ref[:, FF_HIDDEN + c0:FF_HIDDEN + c0 + _FFN_CK])
            act_scr[rs, c0:c0 + _FFN_CK] = (gate * _sigmoid(gate) * up).astype(_BF16)
    for rs in groups:
        x2 = o_ref[rs, :] + _dot(act_scr[rs, :], wfo_ref[...])
        if final_norm:
            x2 = _rms_rows(x2, gfin_ref[...])
        o_ref[rs, :] = x2


def _post_call(x2d, yconv, yattn, gates, wb, wout, gffn, wfi, wfo, gfin, final_norm):
    n_tok = x2d.shape[0]
    tm = _POST_TM
    assert n_tok % tm == 0 and FF_HIDDEN % _FFN_CK == 0 and tm % _POST_GROUPS == 0

    def row_spec(width):
        return pl.BlockSpec((tm, width), lambda i: (i, 0))

    return pl.pallas_call(
        functools.partial(_post_kernel, final_norm=final_norm),
        grid=(n_tok // tm,),
        in_specs=[
            row_spec(D_MODEL), row_spec(CONV_WIDTH), row_spec(ATTN_WIDTH),
            row_spec(N_BRANCH * D_MODEL),
            _resident(wb.shape), _resident(wout.shape), _resident(gffn.shape),
            _resident(wfi.shape), _resident(wfo.shape), _resident(gfin.shape),
        ],
        out_specs=row_spec(D_MODEL),
        out_shape=jax.ShapeDtypeStruct((n_tok, D_MODEL), _F32),
        scratch_shapes=[pltpu.VMEM((tm, FF_HIDDEN), _BF16)],
        compiler_params=pltpu.CompilerParams(
            dimension_semantics=("arbitrary",), vmem_limit_bytes=_VMEM_LIMIT_BYTES),
        name="post",
    )(x2d, yconv, yattn, gates, wb, wout, gffn, wfi, wfo, gfin)


def _split_pairs(w, n_heads):
    lead = w.shape[:-1]
    w = w.reshape(*lead, n_heads, ROPE_HALF, 2)
    return jnp.swapaxes(w, -1, -2).reshape(*lead, n_heads * HEAD_DIM)


def _layer_weights(w_in, conv_w, q_norm_g, k_norm_g, w_branch, w_out, w_ffn_in, w_ffn_out):
    w = w_in.astype(_BF16)
    return dict(
        w=w,
        wqt=_split_pairs(w[:, _IN_Q:_IN_K], N_HEADS).T,
        wk=_split_pairs(w[:, _IN_K:_IN_V], N_KV_HEADS),
        wvt=w[:, _IN_V:_IN_G].T,
        convw=conv_w,
        gq=_split_pairs(q_norm_g, 1).reshape(HEAD_DIM, 1),
        gk=_split_pairs(k_norm_g, 1).reshape(1, HEAD_DIM),
        wb=w_branch.astype(_BF16), wout=w_out.astype(_BF16),
        wfi=w_ffn_in.astype(_BF16), wfo=w_ffn_out.astype(_BF16),
    )


def _rope_table(seq_len):
    n_rows = seq_len // GRID_W
    inv_freq = ROPE_THETA ** (-jnp.arange(0, ROPE_HALF, 2, dtype=_F32) / ROPE_HALF)
    row_ang = jnp.arange(n_rows, dtype=_F32)[:, None] * inv_freq[None, :]
    col_ang = jnp.arange(GRID_W, dtype=_F32)[:, None] * inv_freq[None, :]

    def lanes(ang):
        return jnp.concatenate([jnp.tile(jnp.cos(ang), (1, 2)), jnp.tile(jnp.sin(ang), (1, 2))],
                               axis=-1)

    from_row = (jnp.arange(HEAD_DIM) % ROPE_HALF) < ROPE_HALF // 2
    cs = jnp.where(from_row[None, None, :], lanes(row_ang)[:, None, :], lanes(col_ang)[None, :, :])
    return cs.reshape(seq_len, HEAD_DIM)


def _trunk(x, layers, rope, norm_mix_g, norm_ffn_g, norm_final_g):
    batch, seq_len, _ = x.shape
    x2d = x.reshape(batch * seq_len, D_MODEL)
    gfin = norm_final_g.reshape(1, D_MODEL)
    for l, w in enumerate(layers):
        yconv, qt, k, vt, gates = _proj_call(
            x2d, seq_len, norm_mix_g[l].reshape(1, D_MODEL), w["w"], w["wqt"], w["wk"], w["wvt"],
            w["convw"], w["gq"], w["gk"], rope)
        yattn = _attn_call(qt, k, vt, batch, seq_len)
        x2d = _post_call(x2d, yconv, yattn, gates, w["wb"], w["wout"],
                         norm_ffn_g[l].reshape(1, D_MODEL), w["wfi"], w["wfo"], gfin,
                         final_norm=(l == len(layers) - 1))
    return x2d.reshape(batch, seq_len, D_MODEL)


def kernel(x_prompt, x_sample, norm_mix_g, w_in, conv_w, q_norm_g, k_norm_g, w_branch, w_out,
           norm_ffn_g, w_ffn_in, w_ffn_out, norm_final_g):
    depth = w_in.shape[0]
    layers = [_layer_weights(w_in[l], conv_w[l], q_norm_g[l], k_norm_g[l], w_branch[l], w_out[l],
                             w_ffn_in[l], w_ffn_out[l]) for l in range(depth)]
    rope = _rope_table(max(x_prompt.shape[1], x_sample.shape[1]))
    y_prompt = _trunk(x_prompt, layers, rope, norm_mix_g, norm_ffn_g, norm_final_g)
    y_sample = _trunk(x_sample, layers, rope, norm_mix_g, norm_ffn_g, norm_final_g)
    return (y_prompt, y_sample)
```

```python
import functools

import jax
import jax.numpy as jnp
from jax import lax
from jax.experimental import pallas as pl
from jax.experimental.pallas import tpu as pltpu

D_MODEL = 1024
GRID_W = 64
N_HEADS = 8
N_KV_HEADS = 2
HEAD_DIM = 128
GROUP = N_HEADS // N_KV_HEADS
ATTN_WIDTH = N_HEADS * HEAD_DIM
KV_WIDTH = N_KV_HEADS * HEAD_DIM
ROPE_HALF = HEAD_DIM // 2
ROPE_THETA = 10000.0
CONV_WIDTH = D_MODEL
N_BRANCH = 2
FF_HIDDEN = 2816
EPS = 1e-6

_IN_Q = 3 * CONV_WIDTH
_IN_K = _IN_Q + ATTN_WIDTH
_IN_V = _IN_K + KV_WIDTH
_IN_G = _IN_V + KV_WIDTH

_V7X_VMEM_BYTES = 64 * 1024 * 1024
_VMEM_LIMIT_BYTES = _V7X_VMEM_BYTES - 8 * 1024 * 1024
_HALO = 16

_PROJ_TM = 512
_PROJ_CK = 512
_ATTN_TQ = 512
_ATTN_TKV = 512
_POST_TM = 512
_POST_GROUPS = 2
_FFN_CK = 256

_BF16 = jnp.bfloat16
_F32 = jnp.float32
_TN_DIMS = (((0,), (1,)), ((), ()))
_LOG2_E = 1.4426950408889634


def _dot(a, b):
    return jnp.dot(a, b, preferred_element_type=_F32)


def _dot_tn(a, b):
    return lax.dot_general(a, b, _TN_DIMS, preferred_element_type=_F32)


def _rms_rows(x, g):
    ms = jnp.mean(x * x, axis=-1, keepdims=True)
    return x * lax.rsqrt(ms + EPS) * g


def _sigmoid(x):
    return 0.5 * jnp.tanh(0.5 * x) + 0.5


def _resident(shape):
    nd = len(shape)
    return pl.BlockSpec(shape, lambda *_: (0,) * nd, pipeline_mode=pl.Buffered(1))


def _proj_kernel(xm_ref, xp_ref, xn_ref, gmix_ref, w_ref, wq_ref, wk_ref, wv_ref, convw_ref,
                 gq_ref, gk_ref, cs_ref, yconv_ref, qt_ref, k_ref, vt_ref, g_ref, xn_scr,
                 *, tiles_per_seq):
    tm = xm_ref.shape[0]
    i = pl.program_id(0)
    seq_first = (i % tiles_per_seq) == 0
    seq_last = (i % tiles_per_seq) == tiles_per_seq - 1
    gmix = gmix_ref[...]

    prev = _rms_rows(xp_ref[...], gmix)
    nxt = _rms_rows(xn_ref[...], gmix)
    xn_scr[0:_HALO, :] = jnp.where(seq_first, 0.0, prev).astype(_BF16)
    xn_scr[_HALO:_HALO + tm, :] = _rms_rows(xm_ref[...], gmix).astype(_BF16)
    xn_scr[_HALO + tm:, :] = jnp.where(seq_last, 0.0, nxt).astype(_BF16)

    xe = xn_scr[...]
    xm = xn_scr[_HALO:_HALO + tm, :]
    rows_ext = tm + 2 * _HALO

    for c0 in range(0, CONV_WIDTH, _PROJ_CK):
        cs = slice(c0, c0 + _PROJ_CK)
        u = (_dot(xe, w_ref[:, CONV_WIDTH + c0:CONV_WIDTH + c0 + _PROJ_CK])
             * _dot(xe, w_ref[:, 2 * CONV_WIDTH + c0:2 * CONV_WIDTH + c0 + _PROJ_CK]))
        u_prev = pltpu.roll(u, 1, axis=0)
        u_next = pltpu.roll(u, rows_ext - 1, axis=0)
        w = convw_ref[:, cs]
        conv = w[0:1, :] * u_prev + w[1:2, :] * u + w[2:3, :] * u_next
        cb = _dot(xm, w_ref[:, cs])
        yconv_ref[:, cs] = (cb * conv[_HALO:_HALO + tm, :]).astype(_BF16)

    cs = cs_ref[...]
    cs_t = cs.T
    cos_t = cs_t[:ROPE_HALF, :]
    sin_t = cs_t[ROPE_HALF:, :]
    gq = gq_ref[...]
    q_scale = HEAD_DIM ** -0.5 * _LOG2_E
    qt = _dot_tn(wq_ref[...], xm)
    for h in range(N_HEADS):
        r0 = h * HEAD_DIM
        qh = qt[r0:r0 + HEAD_DIM, :]
        ms = jnp.mean(qh * qh, axis=0, keepdims=True)
        qh = qh * (lax.rsqrt(ms + EPS) * q_scale) * gq
        x0 = qh[:ROPE_HALF, :]
        x1 = qh[ROPE_HALF:, :]
        qt_ref[r0:r0 + ROPE_HALF, :] = (x0 * cos_t - x1 * sin_t).astype(_BF16)
        qt_ref[r0 + ROPE_HALF:r0 + HEAD_DIM, :] = (x0 * sin_t + x1 * cos_t).astype(_BF16)

    kf = _dot(xm, wk_ref[...])
    gk = gk_ref[...]
    sc = pltpu.roll(cs, ROPE_HALF, axis=1)
    first_half = lax.broadcasted_iota(jnp.int32, cs.shape, 1) < ROPE_HALF
    ck = jnp.where(first_half, cs, sc)
    sk = jnp.where(first_half, -sc, cs)
    for h in range(N_KV_HEADS):
        cs = slice(h * HEAD_DIM, (h + 1) * HEAD_DIM)
        kh = _rms_rows(kf[:, cs], gk)
        k_ref[:, cs] = (kh * ck + pltpu.roll(kh, ROPE_HALF, axis=1) * sk).astype(_BF16)

    vt_ref[...] = _dot_tn(wv_ref[...], xm).astype(_BF16)

    for c0 in range(0, N_BRANCH * D_MODEL, _PROJ_CK):
        g_ref[:, c0:c0 + _PROJ_CK] = _sigmoid(
            _dot(xm, w_ref[:, _IN_G + c0:_IN_G + c0 + _PROJ_CK])).astype(_BF16)


def _proj_call(x2d, seq_len, gmix, w, wq, wk, wv, convw, gq, gk, cs):
    n_tok = x2d.shape[0]
    tm = _PROJ_TM
    assert seq_len % tm == 0 and tm % _HALO == 0
    tiles_per_seq = seq_len // tm
    halo_per_tile = tm // _HALO
    n_halo_blocks = n_tok // _HALO

    def prev_map(i):
        return (jnp.maximum(i * halo_per_tile - 1, 0), 0)

    def next_map(i):
        return (jnp.minimum((i + 1) * halo_per_tile, n_halo_blocks - 1), 0)

    in_specs = [
        pl.BlockSpec((tm, D_MODEL), lambda i: (i, 0)),
        pl.BlockSpec((_HALO, D_MODEL), prev_map),
        pl.BlockSpec((_HALO, D_MODEL), next_map),
        _resident(gmix.shape), _resident(w.shape), _resident(wq.shape), _resident(wk.shape),
        _resident(wv.shape),
        _resident(convw.shape), _resident(gq.shape), _resident(gk.shape),
        pl.BlockSpec((tm, HEAD_DIM), lambda i: (i % tiles_per_seq, 0)),
    ]
    out_shape = (
        jax.ShapeDtypeStruct((n_tok, CONV_WIDTH), _BF16),
        jax.ShapeDtypeStruct((ATTN_WIDTH, n_tok), _BF16),
        jax.ShapeDtypeStruct((n_tok, KV_WIDTH), _BF16),
        jax.ShapeDtypeStruct((KV_WIDTH, n_tok), _BF16),
        jax.ShapeDtypeStruct((n_tok, N_BRANCH * D_MODEL), _BF16),
    )
    out_specs = (
        pl.BlockSpec((tm, CONV_WIDTH), lambda i: (i, 0)),
        pl.BlockSpec((ATTN_WIDTH, tm), lambda i: (0, i)),
        pl.BlockSpec((tm, KV_WIDTH), lambda i: (i, 0)),
        pl.BlockSpec((KV_WIDTH, tm), lambda i: (0, i)),
        pl.BlockSpec((tm, N_BRANCH * D_MODEL), lambda i: (i, 0)),
    )
    return pl.pallas_call(
        functools.partial(_proj_kernel, tiles_per_seq=tiles_per_seq),
        grid=(n_tok // tm,),
        in_specs=in_specs,
        out_specs=out_specs,
        out_shape=out_shape,
        scratch_shapes=[pltpu.VMEM((tm + 2 * _HALO, D_MODEL), _BF16)],
        compiler_params=pltpu.CompilerParams(
            dimension_semantics=("arbitrary",), vmem_limit_bytes=_VMEM_LIMIT_BYTES),
        name="proj",
    )(x2d, x2d, x2d, gmix, w, wq, wk, wv, convw, gq, gk, cs)


def _attn_kernel(qt_ref, k_ref, vt_ref, o_ref, acc_scr, s_even, s_odd, mx_even, mx_odd):
    tq = qt_ref.shape[1]
    seq_len = k_ref.shape[0]
    tkv = _ATTN_TKV
    n_chunks = seq_len // tkv
    acc_scr[...] = jnp.zeros_like(acc_scr)

    def kv_slice(c):
        return pl.ds(pl.multiple_of(c * tkv, tkv), tkv)

    def scores(g, kc, s_ref, mx_ref):
        s = _dot(kc, qt_ref[g * HEAD_DIM:(g + 1) * HEAD_DIM, :])
        s_ref[g] = s
        mx_ref[g] = jnp.max(s, axis=0, keepdims=True)

    def softmax_pv(g, vc, s_ref, mx_ref, m, l):
        m_new = jnp.maximum(m, mx_ref[g])
        p = jnp.exp2(s_ref[g] - m_new)
        alpha = jnp.exp2(m - m_new)
        l = alpha * l + jnp.sum(p, axis=0, keepdims=True)
        acc_scr[g] = alpha * acc_scr[g] + _dot(vc, p.astype(_BF16))
        return m_new, l

    def stage(c_next, s_next, mx_next, c_cur, s_cur, mx_cur, state):
        kc = None if c_next is None else k_ref[kv_slice(c_next), :]
        vc = None if c_cur is None else vt_ref[:, kv_slice(c_cur)]
        new = []
        for g in range(GROUP):
            if c_next is not None:
                scores(g, kc, s_next, mx_next)
            if c_cur is not None:
                new.append(softmax_pv(g, vc, s_cur, mx_cur, *state[g]))
        return tuple(new) if new else state

    def body(j, state):
        c = 2 * j
        state = stage(c + 1, s_odd, mx_odd, c, s_even, mx_even, state)
        return stage(c + 2, s_even, mx_even, c + 1, s_odd, mx_odd, state)

    state = tuple((jnp.full((1, tq), -1e30, _F32), jnp.zeros((1, tq), _F32))
                  for _ in range(GROUP))
    state = stage(0, s_even, mx_even, None, None, None, state)
    state = lax.fori_loop(0, n_chunks // 2 - 1, body, state)
    state = stage(n_chunks - 1, s_odd, mx_odd, n_chunks - 2, s_even, mx_even, state)
    state = stage(None, None, None, n_chunks - 1, s_odd, mx_odd, state)
    for g in range(GROUP):
        o_ref[:, g * HEAD_DIM:(g + 1) * HEAD_DIM] = (acc_scr[g] / state[g][1]).T.astype(_BF16)


def _attn_call(qt, k, vt, batch, seq_len):
    n_tok = batch * seq_len
    tq = _ATTN_TQ
    assert seq_len % tq == 0 and seq_len % (2 * _ATTN_TKV) == 0
    q_tiles = seq_len // tq
    group_w = GROUP * HEAD_DIM
    return pl.pallas_call(
        _attn_kernel,
        grid=(batch, N_KV_HEADS, q_tiles),
        in_specs=[
            pl.BlockSpec((group_w, tq), lambda b, h, i: (h, b * q_tiles + i)),
            pl.BlockSpec((seq_len, HEAD_DIM), lambda b, h, i: (b, h)),
            pl.BlockSpec((HEAD_DIM, seq_len), lambda b, h, i: (h, b)),
        ],
        out_specs=pl.BlockSpec((tq, group_w), lambda b, h, i: (b * q_tiles + i, h)),
        out_shape=jax.ShapeDtypeStruct((n_tok, ATTN_WIDTH), _BF16),
        scratch_shapes=[pltpu.VMEM((GROUP, HEAD_DIM, tq), _F32),
                        pltpu.VMEM((GROUP, _ATTN_TKV, tq), _F32),
                        pltpu.VMEM((GROUP, _ATTN_TKV, tq), _F32),
                        pltpu.VMEM((GROUP, 1, tq), _F32),
                        pltpu.VMEM((GROUP, 1, tq), _F32)],
        compiler_params=pltpu.CompilerParams(
            dimension_semantics=("arbitrary", "arbitrary", "arbitrary"),
            vmem_limit_bytes=_VMEM_LIMIT_BYTES),
        name="attn",
    )(qt, k, vt)


def _post_kernel(x_ref, yc_ref, ya_ref, g_ref, wb_ref, wout_ref, gffn_ref, wfi_ref, wfo_ref,
                 gfin_ref, o_ref, act_scr, *, final_norm):
    tm = x_ref.shape[0]
    groups = [slice(r0, r0 + tm // _POST_GROUPS) for r0 in range(0, tm, tm // _POST_GROUPS)]

    merged = []
    for rs in groups:
        g = g_ref[rs, :].astype(_F32)
        m = g[:, :D_MODEL] * _dot(yc_ref[rs, :], wb_ref[0])
        m += g[:, D_MODEL:] * _dot(ya_ref[rs, :], wb_ref[1])
        merged.append(m.astype(_BF16))
    xn = []
    for rs, m in zip(groups, merged):
        x1 = x_ref[rs, :] + _dot(m, wout_ref[...])
        o_ref[rs, :] = x1
        xn.append(_rms_rows(x1, gffn_ref[...]).astype(_BF16))
    for c0 in range(0, FF_HIDDEN, _FFN_CK):
        for rs, xg in zip(groups, xn):
            gate = _dot(xg, wfi_ref[:, c0:c0 + _FFN_CK])
            up = _dot(xg, wfi_ref[:, FF_HIDDEN + c0:FF_HIDDEN + c0 + _FFN_CK])
            act_scr[rs, c0:c0 + _FFN_CK] = (gate * _sigmoid(gate) * up).astype(_BF16)
    for rs in groups:
        x2 = o_ref[rs, :] + _dot(act_scr[rs, :], wfo_ref[...])
        if final_norm:
            x2 = _rms_rows(x2, gfin_ref[...])
        o_ref[rs, :] = x2


def _post_call(x2d, yconv, yattn, gates, wb, wout, gffn, wfi, wfo, gfin, final_norm):
    n_tok = x2d.shape[0]
    tm = _POST_TM
    assert n_tok % tm == 0 and FF_HIDDEN % _FFN_CK == 0 and tm % _POST_GROUPS == 0

    def row_spec(width):
        return pl.BlockSpec((tm, width), lambda i: (i, 0))

    return pl.pallas_call(
        functools.partial(_post_kernel, final_norm=final_norm),
        grid=(n_tok // tm,),
        in_specs=[
            row_spec(D_MODEL), row_spec(CONV_WIDTH), row_spec(ATTN_WIDTH),
            row_spec(N_BRANCH * D_MODEL),
            _resident(wb.shape), _resident(wout.shape), _resident(gffn.shape),
            _resident(wfi.shape), _resident(wfo.shape), _resident(gfin.shape),
        ],
        out_specs=row_spec(D_MODEL),
        out_shape=jax.ShapeDtypeStruct((n_tok, D_MODEL), _F32),
        scratch_shapes=[pltpu.VMEM((tm, FF_HIDDEN), _BF16)],
        compiler_params=pltpu.CompilerParams(
            dimension_semantics=("arbitrary",), vmem_limit_bytes=_VMEM_LIMIT_BYTES),
        name="post",
    )(x2d, yconv, yattn, gates, wb, wout, gffn, wfi, wfo, gfin)


def _split_pairs(w, n_heads):
    lead = w.shape[:-1]
    w = w.reshape(*lead, n_heads, ROPE_HALF, 2)
    return jnp.swapaxes(w, -1, -2).reshape(*lead, n_heads * HEAD_DIM)


def _layer_weights(w_in, conv_w, q_norm_g, k_norm_g, w_branch, w_out, w_ffn_in, w_ffn_out):
    return dict(
        w=w_in.astype(_BF16),
        wq=_split_pairs(w_in[:, _IN_Q:_IN_K], N_HEADS).astype(_BF16),
        wk=_split_pairs(w_in[:, _IN_K:_IN_V], N_KV_HEADS).astype(_BF16),
        wv=w_in[:, _IN_V:_IN_G].astype(_BF16),
        convw=conv_w,
        gq=_split_pairs(q_norm_g, 1).reshape(HEAD_DIM, 1),
        gk=_split_pairs(k_norm_g, 1).reshape(1, HEAD_DIM),
        wb=w_branch.astype(_BF16), wout=w_out.astype(_BF16),
        wfi=w_ffn_in.astype(_BF16), wfo=w_ffn_out.astype(_BF16),
    )


def _rope_table(seq_len):
    n_rows = seq_len // GRID_W
    inv_freq = ROPE_THETA ** (-jnp.arange(0, ROPE_HALF, 2, dtype=_F32) / ROPE_HALF)
    row_ang = jnp.arange(n_rows, dtype=_F32)[:, None] * inv_freq[None, :]
    col_ang = jnp.arange(GRID_W, dtype=_F32)[:, None] * inv_freq[None, :]

    def lanes(ang):
        return jnp.concatenate([jnp.tile(jnp.cos(ang), (1, 2)), jnp.tile(jnp.sin(ang), (1, 2))],
                               axis=-1)

    from_row = (jnp.arange(HEAD_DIM) % ROPE_HALF) < ROPE_HALF // 2
    cs = jnp.where(from_row[None, None, :], lanes(row_ang)[:, None, :], lanes(col_ang)[None, :, :])
    return cs.reshape(seq_len, HEAD_DIM)


def _trunk(x, layers, rope, norm_mix_g, norm_ffn_g, norm_final_g):
    batch, seq_len, _ = x.shape
    x2d = x.reshape(batch * seq_len, D_MODEL)
    gfin = norm_final_g.reshape(1, D_MODEL)
    for l, w in enumerate(layers):
        yconv, qt, k, vt, gates = _proj_call(
            x2d, seq_len, norm_mix_g[l].reshape(1, D_MODEL), w["w"], w["wq"], w["wk"], w["wv"],
            w["convw"], w["gq"], w["gk"], rope)
        yattn = _attn_call(qt, k, vt, batch, seq_len)
        x2d = _post_call(x2d, yconv, yattn, gates, w["wb"], w["wout"],
                         norm_ffn_g[l].reshape(1, D_MODEL), w["wfi"], w["wfo"], gfin,
                         final_norm=(l == len(layers) - 1))
    return x2d.reshape(batch, seq_len, D_MODEL)


def kernel(x_prompt, x_sample, norm_mix_g, w_in, conv_w, q_norm_g, k_norm_g, w_branch, w_out,
           norm_ffn_g, w_ffn_in, w_ffn_out, norm_final_g):
    depth = w_in.shape[0]
    layers = [_layer_weights(w_in[l], conv_w[l], q_norm_g[l], k_norm_g[l], w_branch[l], w_out[l],
                             w_ffn_in[l], w_ffn_out[l]) for l in range(depth)]
    rope = _rope_table(max(x_prompt.shape[1], x_sample.shape[1]))
    y_prompt = _trunk(x_prompt, layers, rope, norm_mix_g, norm_ffn_g, norm_final_g)
    y_sample = _trunk(x_sample, layers, rope, norm_mix_g, norm_ffn_g, norm_final_g)
    return (y_prompt, y_sample)
```

```python
import functools

import jax
import jax.numpy as jnp
from jax import lax
from jax.experimental import pallas as pl
from jax.experimental.pallas import tpu as pltpu

D_MODEL = 1024
GRID_W = 64
N_HEADS = 8
N_KV_HEADS = 2
HEAD_DIM = 128
GROUP = N_HEADS // N_KV_HEADS
ATTN_WIDTH = N_HEADS * HEAD_DIM
KV_WIDTH = N_KV_HEADS * HEAD_DIM
ROPE_HALF = HEAD_DIM // 2
ROPE_THETA = 10000.0
CONV_WIDTH = D_MODEL
N_BRANCH = 2
FF_HIDDEN = 2816
EPS = 1e-6

_IN_Q = 3 * CONV_WIDTH
_IN_K = _IN_Q + ATTN_WIDTH
_IN_V = _IN_K + KV_WIDTH
_IN_G = _IN_V + KV_WIDTH

_V7X_VMEM_BYTES = 64 * 1024 * 1024
_VMEM_LIMIT_BYTES = _V7X_VMEM_BYTES - 8 * 1024 * 1024
_HALO = 16

_PROJ_TM = 1024
_PROJ_CK = 512
_ATTN_TQ = 512
_ATTN_TKV = 512
_POST_TM = 512
_POST_GROUPS = 2
_FFN_CK = 256

_BF16 = jnp.bfloat16
_F32 = jnp.float32
_TN_DIMS = (((0,), (1,)), ((), ()))
_LOG2_E = 1.4426950408889634


def _dot(a, b):
    return jnp.dot(a, b, preferred_element_type=_F32)


def _dot_tn(a, b):
    return lax.dot_general(a, b, _TN_DIMS, preferred_element_type=_F32)


def _rms_rows(x, g):
    ms = jnp.mean(x * x, axis=-1, keepdims=True)
    return x * lax.rsqrt(ms + EPS) * g


def _sigmoid(x):
    return 0.5 * jnp.tanh(0.5 * x) + 0.5


def _resident(shape):
    nd = len(shape)
    return pl.BlockSpec(shape, lambda *_: (0,) * nd, pipeline_mode=pl.Buffered(1))


def _proj_kernel(xm_ref, xp_ref, xn_ref, gmix_ref, w_ref, wq_ref, wk_ref, wv_ref, convw_ref,
                 gq_ref, gk_ref, cs_ref, yconv_ref, qt_ref, k_ref, vt_ref, g_ref, xn_scr,
                 *, tiles_per_seq):
    tm = xm_ref.shape[0]
    i = pl.program_id(0)
    seq_first = (i % tiles_per_seq) == 0
    seq_last = (i % tiles_per_seq) == tiles_per_seq - 1
    gmix = gmix_ref[...]

    prev = _rms_rows(xp_ref[...], gmix)
    nxt = _rms_rows(xn_ref[...], gmix)
    xn_scr[0:_HALO, :] = jnp.where(seq_first, 0.0, prev).astype(_BF16)
    xn_scr[_HALO:_HALO + tm, :] = _rms_rows(xm_ref[...], gmix).astype(_BF16)
    xn_scr[_HALO + tm:, :] = jnp.where(seq_last, 0.0, nxt).astype(_BF16)

    xe = xn_scr[...]
    xm = xn_scr[_HALO:_HALO + tm, :]
    rows_ext = tm + 2 * _HALO

    for c0 in range(0, CONV_WIDTH, _PROJ_CK):
        cs = slice(c0, c0 + _PROJ_CK)
        u = (_dot(xe, w_ref[:, CONV_WIDTH + c0:CONV_WIDTH + c0 + _PROJ_CK])
             * _dot(xe, w_ref[:, 2 * CONV_WIDTH + c0:2 * CONV_WIDTH + c0 + _PROJ_CK]))
        u_prev = pltpu.roll(u, 1, axis=0)
        u_next = pltpu.roll(u, rows_ext - 1, axis=0)
        w = convw_ref[:, cs]
        conv = w[0:1, :] * u_prev + w[1:2, :] * u + w[2:3, :] * u_next
        cb = _dot(xm, w_ref[:, cs])
        yconv_ref[:, cs] = (cb * conv[_HALO:_HALO + tm, :]).astype(_BF16)

    cs = cs_ref[...]
    cs_t = cs.T
    cos_t = cs_t[:ROPE_HALF, :]
    sin_t = cs_t[ROPE_HALF:, :]
    gq = gq_ref[...]
    q_scale = HEAD_DIM ** -0.5 * _LOG2_E
    qt = _dot_tn(wq_ref[...], xm)
    for h in range(N_HEADS):
        r0 = h * HEAD_DIM
        qh = qt[r0:r0 + HEAD_DIM, :]
        ms = jnp.mean(qh * qh, axis=0, keepdims=True)
        qh = qh * (lax.rsqrt(ms + EPS) * q_scale) * gq
        x0 = qh[:ROPE_HALF, :]
        x1 = qh[ROPE_HALF:, :]
        qt_ref[r0:r0 + ROPE_HALF, :] = (x0 * cos_t - x1 * sin_t).astype(_BF16)
        qt_ref[r0 + ROPE_HALF:r0 + HEAD_DIM, :] = (x0 * sin_t + x1 * cos_t).astype(_BF16)

    kf = _dot(xm, wk_ref[...])
    gk = gk_ref[...]
    sc = pltpu.roll(cs, ROPE_HALF, axis=1)
    first_half = lax.broadcasted_iota(jnp.int32, cs.shape, 1) < ROPE_HALF
    ck = jnp.where(first_half, cs, sc)
    sk = jnp.where(first_half, -sc, cs)
    for h in range(N_KV_HEADS):
        cs = slice(h * HEAD_DIM, (h + 1) * HEAD_DIM)
        kh = _rms_rows(kf[:, cs], gk)
        k_ref[:, cs] = (kh * ck + pltpu.roll(kh, ROPE_HALF, axis=1) * sk).astype(_BF16)

    vt_ref[...] = _dot_tn(wv_ref[...], xm).astype(_BF16)

    for c0 in range(0, N_BRANCH * D_MODEL, _PROJ_CK):
        g_ref[:, c0:c0 + _PROJ_CK] = _sigmoid(
            _dot(xm, w_ref[:, _IN_G + c0:_IN_G + c0 + _PROJ_CK])).astype(_BF16)


def _proj_call(x2d, seq_len, gmix, w, wq, wk, wv, convw, gq, gk, cs):
    n_tok = x2d.shape[0]
    tm = _PROJ_TM
    assert seq_len % tm == 0 and tm % _HALO == 0
    tiles_per_seq = seq_len // tm
    halo_per_tile = tm // _HALO
    n_halo_blocks = n_tok // _HALO

    def prev_map(i):
        return (jnp.maximum(i * halo_per_tile - 1, 0), 0)

    def next_map(i):
        return (jnp.minimum((i + 1) * halo_per_tile, n_halo_blocks - 1), 0)

    in_specs = [
        pl.BlockSpec((tm, D_MODEL), lambda i: (i, 0)),
        pl.BlockSpec((_HALO, D_MODEL), prev_map),
        pl.BlockSpec((_HALO, D_MODEL), next_map),
        _resident(gmix.shape), _resident(w.shape), _resident(wq.shape), _resident(wk.shape),
        _resident(wv.shape),
        _resident(convw.shape), _resident(gq.shape), _resident(gk.shape),
        pl.BlockSpec((tm, HEAD_DIM), lambda i: (i % tiles_per_seq, 0)),
    ]
    out_shape = (
        jax.ShapeDtypeStruct((n_tok, CONV_WIDTH), _BF16),
        jax.ShapeDtypeStruct((ATTN_WIDTH, n_tok), _BF16),
        jax.ShapeDtypeStruct((n_tok, KV_WIDTH), _BF16),
        jax.ShapeDtypeStruct((KV_WIDTH, n_tok), _BF16),
        jax.ShapeDtypeStruct((n_tok, N_BRANCH * D_MODEL), _BF16),
    )
    out_specs = (
        pl.BlockSpec((tm, CONV_WIDTH), lambda i: (i, 0)),
        pl.BlockSpec((ATTN_WIDTH, tm), lambda i: (0, i)),
        pl.BlockSpec((tm, KV_WIDTH), lambda i: (i, 0)),
        pl.BlockSpec((KV_WIDTH, tm), lambda i: (0, i)),
        pl.BlockSpec((tm, N_BRANCH * D_MODEL), lambda i: (i, 0)),
    )
    return pl.pallas_call(
        functools.partial(_proj_kernel, tiles_per_seq=tiles_per_seq),
        grid=(n_tok // tm,),
        in_specs=in_specs,
        out_specs=out_specs,
        out_shape=out_shape,
        scratch_shapes=[pltpu.VMEM((tm + 2 * _HALO, D_MODEL), _BF16)],
        compiler_params=pltpu.CompilerParams(
            dimension_semantics=("arbitrary",), vmem_limit_bytes=_VMEM_LIMIT_BYTES),
        name="proj",
    )(x2d, x2d, x2d, gmix, w, wq, wk, wv, convw, gq, gk, cs)


def _attn_kernel(qt_ref, k_ref, vt_ref, o_ref, acc_scr, s_even, s_odd, mx_even, mx_odd):
    tq = qt_ref.shape[1]
    seq_len = k_ref.shape[0]
    tkv = _ATTN_TKV
    n_chunks = seq_len // tkv
    acc_scr[...] = jnp.zeros_like(acc_scr)

    def kv_slice(c):
        return pl.ds(pl.multiple_of(c * tkv, tkv), tkv)

    def scores(g, kc, s_ref, mx_ref):
        s = _dot(kc, qt_ref[g * HEAD_DIM:(g + 1) * HEAD_DIM, :])
        s_ref[g] = s
        mx_ref[g] = jnp.max(s, axis=0, keepdims=True)

    def softmax_pv(g, vc, s_ref, mx_ref, m, l):
        m_new = jnp.maximum(m, mx_ref[g])
        p = jnp.exp2(s_ref[g] - m_new)
        alpha = jnp.exp2(m - m_new)
        l = alpha * l + jnp.sum(p, axis=0, keepdims=True)
        acc_scr[g] = alpha * acc_scr[g] + _dot(vc, p.astype(_BF16))
        return m_new, l

    def stage(c_next, s_next, mx_next, c_cur, s_cur, mx_cur, state):
        kc = None if c_next is None else k_ref[kv_slice(c_next), :]
        vc = None if c_cur is None else vt_ref[:, kv_slice(c_cur)]
        new = []
        for g in range(GROUP):
            if c_next is not None:
                scores(g, kc, s_next, mx_next)
            if c_cur is not None:
                new.append(softmax_pv(g, vc, s_cur, mx_cur, *state[g]))
        return tuple(new) if new else state

    def body(j, state):
        c = 2 * j
        state = stage(c + 1, s_odd, mx_odd, c, s_even, mx_even, state)
        return stage(c + 2, s_even, mx_even, c + 1, s_odd, mx_odd, state)

    state = tuple((jnp.full((1, tq), -1e30, _F32), jnp.zeros((1, tq), _F32))
                  for _ in range(GROUP))
    state = stage(0, s_even, mx_even, None, None, None, state)
    state = lax.fori_loop(0, n_chunks // 2 - 1, body, state)
    state = stage(n_chunks - 1, s_odd, mx_odd, n_chunks - 2, s_even, mx_even, state)
    state = stage(None, None, None, n_chunks - 1, s_odd, mx_odd, state)
    for g in range(GROUP):
        o_ref[:, g * HEAD_DIM:(g + 1) * HEAD_DIM] = (acc_scr[g] / state[g][1]).T.astype(_BF16)


def _attn_call(qt, k, vt, batch, seq_len):
    n_tok = batch * seq_len
    tq = _ATTN_TQ
    assert seq_len % tq == 0 and seq_len % (2 * _ATTN_TKV) == 0
    q_tiles = seq_len // tq
    group_w = GROUP * HEAD_DIM
    return pl.pallas_call(
        _attn_kernel,
        grid=(batch, N_KV_HEADS, q_tiles),
        in_specs=[
            pl.BlockSpec((group_w, tq), lambda b, h, i: (h, b * q_tiles + i)),
            pl.BlockSpec((seq_len, HEAD_DIM), lambda b, h, i: (b, h)),
            pl.BlockSpec((HEAD_DIM, seq_len), lambda b, h, i: (h, b)),
        ],
        out_specs=pl.BlockSpec((tq, group_w), lambda b, h, i: (b * q_tiles + i, h)),
        out_shape=jax.ShapeDtypeStruct((n_tok, ATTN_WIDTH), _BF16),
        scratch_shapes=[pltpu.VMEM((GROUP, HEAD_DIM, tq), _F32),
                        pltpu.VMEM((GROUP, _ATTN_TKV, tq), _F32),
                        pltpu.VMEM((GROUP, _ATTN_TKV, tq), _F32),
                        pltpu.VMEM((GROUP, 1, tq), _F32),
                        pltpu.VMEM((GROUP, 1, tq), _F32)],
        compiler_params=pltpu.CompilerParams(
            dimension_semantics=("arbitrary", "arbitrary", "arbitrary"),
            vmem_limit_bytes=_VMEM_LIMIT_BYTES),
        name="attn",
    )(qt, k, vt)


def _post_kernel(x_ref, yc_ref, ya_ref, g_ref, wb_ref, wout_ref, gffn_ref, wfi_ref, wfo_ref,
                 gfin_ref, o_ref, act_scr, *, final_norm):
    tm = x_ref.shape[0]
    groups = [slice(r0, r0 + tm // _POST_GROUPS) for r0 in range(0, tm, tm // _POST_GROUPS)]

    merged = []
    for rs in groups:
        g = g_ref[rs, :].astype(_F32)
        m = g[:, :D_MODEL] * _dot(yc_ref[rs, :], wb_ref[0])
        m += g[:, D_MODEL:] * _dot(ya_ref[rs, :], wb_ref[1])
        merged.append(m.astype(_BF16))
    xn = []
    for rs, m in zip(groups, merged):
        x1 = x_ref[rs, :] + _dot(m, wout_ref[...])
        o_ref[rs, :] = x1
        xn.append(_rms_rows(x1, gffn_ref[...]).astype(_BF16))
    for c0 in range(0, FF_HIDDEN, _FFN_CK):
        for rs, xg in zip(groups, xn):
            gate = _dot(xg, wfi_ref[:, c0:c0 + _FFN_CK])
            up = _dot(xg, wfi_ref[:, FF_HIDDEN + c0:FF_HIDDEN + c0 + _FFN_CK])
            act_scr[rs, c0:c0 + _FFN_CK] = (gate * _sigmoid(gate) * up).astype(_BF16)
    for rs in groups:
        x2 = o_ref[rs, :] + _dot(act_scr[rs, :], wfo_ref[...])
        if final_norm:
            x2 = _rms_rows(x2, gfin_ref[...])
        o_ref[rs, :] = x2


def _post_call(x2d, yconv, yattn, gates, wb, wout, gffn, wfi, wfo, gfin, final_norm):
    n_tok = x2d.shape[0]
    tm = _POST_TM
    assert n_tok % tm == 0 and FF_HIDDEN % _FFN_CK == 0 and tm % _POST_GROUPS == 0

    def row_spec(width):
        return pl.BlockSpec((tm, width), lambda i: (i, 0))

    return pl.pallas_call(
        functools.partial(_post_kernel, final_norm=final_norm),
        grid=(n_tok // tm,),
        in_specs=[
            row_spec(D_MODEL), row_spec(CONV_WIDTH), row_spec(ATTN_WIDTH),
            row_spec(N_BRANCH * D_MODEL),
            _resident(wb.shape), _resident(wout.shape), _resident(gffn.shape),
            _resident(wfi.shape), _resident(wfo.shape), _resident(gfin.shape),
        ],
        out_specs=row_spec(D_MODEL),
        out_shape=jax.ShapeDtypeStruct((n_tok, D_MODEL), _F32),
        scratch_shapes=[pltpu.VMEM((tm, FF_HIDDEN), _BF16)],
        compiler_params=pltpu.CompilerParams(
            dimension_semantics=("arbitrary",), vmem_limit_bytes=_VMEM_LIMIT_BYTES),
        name="post",
    )(x2d, yconv, yattn, gates, wb, wout, gffn, wfi, wfo, gfin)


def _split_pairs(w, n_heads):
    lead = w.shape[:-1]
    w = w.reshape(*lead, n_heads, ROPE_HALF, 2)
    return jnp.swapaxes(w, -1, -2).reshape(*lead, n_heads * HEAD_DIM)


def _layer_weights(w_in, conv_w, q_norm_g, k_norm_g, w_branch, w_out, w_ffn_in, w_ffn_out):
    return dict(
        w=w_in.astype(_BF16),
        wq=_split_pairs(w_in[:, _IN_Q:_IN_K], N_HEADS).astype(_BF16),
        wk=_split_pairs(w_in[:, _IN_K:_IN_V], N_KV_HEADS).astype(_BF16),
        wv=w_in[:, _IN_V:_IN_G].astype(_BF16),
        convw=conv_w,
        gq=_split_pairs(q_norm_g, 1).reshape(HEAD_DIM, 1),
        gk=_split_pairs(k_norm_g, 1).reshape(1, HEAD_DIM),
        wb=w_branch.astype(_BF16), wout=w_out.astype(_BF16),
        wfi=w_ffn_in.astype(_BF16), wfo=w_ffn_out.astype(_BF16),
    )


def _rope_table(seq_len):
    n_rows = seq_len // GRID_W
    inv_freq = ROPE_THETA ** (-jnp.arange(0, ROPE_HALF, 2, dtype=_F32) / ROPE_HALF)
    row_ang = jnp.arange(n_rows, dtype=_F32)[:, None] * inv_freq[None, :]
    col_ang = jnp.arange(GRID_W, dtype=_F32)[:, None] * inv_freq[None, :]

    def lanes(ang):
        return jnp.concatenate([jnp.tile(jnp.cos(ang), (1, 2)), jnp.tile(jnp.sin(ang), (1, 2))],
                               axis=-1)

    from_row = (jnp.arange(HEAD_DIM) % ROPE_HALF) < ROPE_HALF // 2
    cs = jnp.where(from_row[None, None, :], lanes(row_ang)[:, None, :], lanes(col_ang)[None, :, :])
    return cs.reshape(seq_len, HEAD_DIM)


def _trunk(x, layers, rope, norm_mix_g, norm_ffn_g, norm_final_g):
    batch, seq_len, _ = x.shape
    x2d = x.reshape(batch * seq_len, D_MODEL)
    gfin = norm_final_g.reshape(1, D_MODEL)
    for l, w in enumerate(layers):
        yconv, qt, k, vt, gates = _proj_call(
            x2d, seq_len, norm_mix_g[l].reshape(1, D_MODEL), w["w"], w["wq"], w["wk"], w["wv"],
            w["convw"], w["gq"], w["gk"], rope)
        yattn = _attn_call(qt, k, vt, batch, seq_len)
        x2d = _post_call(x2d, yconv, yattn, gates, w["wb"], w["wout"],
                         norm_ffn_g[l].reshape(1, D_MODEL), w["wfi"], w["wfo"], gfin,
                         final_norm=(l == len(layers) - 1))
    return x2d.reshape(batch, seq_len, D_MODEL)


def kernel(x_prompt, x_sample, norm_mix_g, w_in, conv_w, q_norm_g, k_norm_g, w_branch, w_out,
           norm_ffn_g, w_ffn_in, w_ffn_out, norm_final_g):
    depth = w_in.shape[0]
    layers = [_layer_weights(w_in[l], conv_w[l], q_norm_g[l], k_norm_g[l], w_branch[l], w_out[l],
                             w_ffn_in[l], w_ffn_out[l]) for l in range(depth)]
    rope = _rope_table(max(x_prompt.shape[1], x_sample.shape[1]))
    y_prompt = _trunk(x_prompt, layers, rope, norm_mix_g, norm_ffn_g, norm_final_g)
    y_sample = _trunk(x_sample, layers, rope, norm_mix_g, norm_ffn_g, norm_final_g)
    return (y_prompt, y_sample)
```
